```python
import jax
import jax.numpy as jnp
from jax import lax
import numpy as np

D_MODEL = 4096
BATCH = 4
SEQ = 2048
DEPTH = 1
DEC_BATCH = 128
DEC_SEQ = 4
PAST_LEN = 16384
PAGE_SIZE = 128

N_MOD = 6
EPS = 1e-6
POOL_WINDOWS = (2, 4, 8, 16)
N_POOL_GROUPS = 4
POOL_WIDTH = D_MODEL // 2
POOL_GW = POOL_WIDTH // N_POOL_GROUPS
POOL_BUF = 15
M_HEADS = 8
M_DK = D_MODEL // (2 * M_HEADS)
M_DV = D_MODEL // M_HEADS
M_QK = M_HEADS * M_DK
M_V = M_HEADS * M_DV
CHUNK = 64
GATE_CAP = 15.0
IN_SIZES = (POOL_WIDTH, M_QK, M_QK, M_V, M_V, M_HEADS, M_HEADS, D_MODEL, D_MODEL)
N_IN = sum(IN_SIZES)
N_EXPERTS = 32
TOP_K = 4
D_EXPERT = D_MODEL
SWIGLU_LIMIT = 7.0
SWIGLU_ALPHA = 1.702
EXPERT_BLOCK = 128

kernel_name = 'adaln_pool_mlstm_moe_decode_step'


def _split_points():
    pts, acc = [], 0
    for s in IN_SIZES[:-1]:
        acc += s
        pts.append(acc)
    return pts


def rms_norm(x, g):
    xf = x.astype(jnp.float32)
    y = xf * lax.rsqrt(jnp.mean(xf * xf, axis=-1, keepdims=True) + EPS)
    return (y * g.astype(jnp.float32)).astype(x.dtype)


def pool_mixer(u, prev, pos, w_pool_grp, pool_scale):
    B, T, _ = u.shape
    padded = jnp.concatenate([prev.astype(u.dtype), u], axis=1).astype(jnp.float32)
    cs = jnp.cumsum(padded, axis=1)
    cs = jnp.concatenate([jnp.zeros((B, 1, POOL_WIDTH), jnp.float32), cs], axis=1)
    end = cs[:, POOL_BUF + 1:, :]
    means = []
    for g, w in enumerate(POOL_WINDOWS):
        lo, hi = g * POOL_GW, (g + 1) * POOL_GW
        start = cs[:, POOL_BUF + 1 - w:POOL_BUF + 1 - w + T, lo:hi]
        cnt = jnp.minimum(pos + 1, w).astype(jnp.float32)[None, :, None]
        means.append((end[:, :, lo:hi] - start) / cnt)
    d = jnp.concatenate(means, axis=-1) - padded[:, POOL_BUF:, :]
    d = d.astype(u.dtype).reshape(B, T, N_POOL_GROUPS, POOL_GW)
    a = jnp.einsum('btgc,gcd->btgd', d, w_pool_grp).reshape(B, T, POOL_WIDTH) * pool_scale
    return a, padded[:, -POOL_BUF:, :]


def mlstm_scan(q, k, v, ig, lf, C0, n0, m0):
    B, T, H, _ = q.shape
    L = CHUNK if T % CHUNK == 0 else T
    nc = T // L

    def to_chunks(a):
        a = a.astype(jnp.float32).reshape((B, nc, L) + a.shape[2:])
        perm = (1, 0, 3, 2) + tuple(range(4, a.ndim))
        return jnp.transpose(a, perm)

    causal = jnp.tril(jnp.ones((L, L), bool))

    def step(carry, inp):
        C, n, m = carry
        qc, kc, vc, ic, fc = inp
        b = jnp.cumsum(fc, axis=-1)
        dmat = jnp.where(causal, b[..., :, None] - b[..., None, :] + ic[..., None, :], -jnp.inf)
        inter = b + m[..., None]
        mt = jnp.maximum(inter, jnp.max(dmat, axis=-1))
        s = jnp.einsum('bhtd,bhsd->bhts', qc, kc) * jnp.exp(dmat - mt[..., None])
        w_inter = jnp.exp(inter - mt)
        num = w_inter[..., None] * jnp.einsum('bhtd,bhde->bhte', qc, C) + jnp.einsum('bhts,bhse->bhte', s, vc)
        den = w_inter * jnp.einsum('bhtd,bhd->bht', qc, n) + jnp.sum(s, axis=-1)
        h = num / jnp.maximum(jnp.abs(den), jnp.exp(-mt))[..., None]
        bl = b[..., -1]
        wlog = bl[..., None] - b + ic
        m_new = jnp.maximum(bl + m, jnp.max(wlog, axis=-1))
        decay = jnp.exp(bl + m - m_new)
        kw = kc * jnp.exp(wlog - m_new[..., None])[..., None]
        C_new = decay[..., None, None] * C + jnp.einsum('bhsd,bhse->bhde', kw, vc)
        n_new = decay[..., None] * n + jnp.sum(kw, axis=2)
        return (C_new, n_new, m_new), h

    init = (C0.astype(jnp.float32), n0.astype(jnp.float32), m0.astype(jnp.float32))
    (C, n, m), h = lax.scan(step, init, (to_chunks(q), to_chunks(k), to_chunks(v), to_chunks(ig), to_chunks(lf)))
    h = jnp.transpose(h, (1, 0, 3, 2, 4)).reshape(B, T, H, -1)
    return h, C, n, m


def moe(h, w_router, b_router, w_up, b_up, w_down, b_down):
    T, D = h.shape
    logits = h.astype(jnp.float32) @ w_router.astype(jnp.float32) + b_router.astype(jnp.float32)
    top_val, top_idx = lax.top_k(logits, TOP_K)
    gates = jax.nn.softmax(top_val, axis=-1)
    A = T * TOP_K
    flat_e = top_idx.reshape(A)
    order = jnp.argsort(flat_e)
    sorted_e = flat_e[order]
    counts = jnp.bincount(flat_e, length=N_EXPERTS)
    padded_counts = ((counts + EXPERT_BLOCK - 1) // EXPERT_BLOCK) * EXPERT_BLOCK
    pad_end = jnp.cumsum(padded_counts)
    pad_start = pad_end - padded_counts
    start = jnp.cumsum(counts) - counts
    dest = pad_start[sorted_e] + (jnp.arange(A) - start[sorted_e])
    n_blocks = -(-A // EXPERT_BLOCK) + N_EXPERTS
    P = n_blocks * EXPERT_BLOCK
    slot = jnp.full((P,), A, jnp.int32).at[dest].set(order.astype(jnp.int32))
    tok = jnp.concatenate([jnp.arange(A, dtype=jnp.int32) // TOP_K, jnp.array([T], jnp.int32)])[slot]
    gate_rows = jnp.concatenate([gates.reshape(A), jnp.zeros((1,), jnp.float32)])[slot]
    block_e = jnp.minimum(jnp.searchsorted(pad_end, jnp.arange(n_blocks) * EXPERT_BLOCK, side='right'), N_EXPERTS - 1)
    h_pad = jnp.concatenate([h, jnp.zeros((1, D), h.dtype)], axis=0)
    xb = h_pad[tok].reshape(n_blocks, EXPERT_BLOCK, D)

    def expert_block(args):
        xblk, e = args
        up = xblk @ w_up[e] + b_up[e]
        glu = jnp.minimum(up[:, :D_EXPERT], SWIGLU_LIMIT)
        lin = jnp.clip(up[:, D_EXPERT:], -SWIGLU_LIMIT, SWIGLU_LIMIT)
        act = glu * jax.nn.sigmoid(SWIGLU_ALPHA * glu) * (lin + 1)
        return act @ w_down[e] + b_down[e]

    yb = lax.map(expert_block, (xb, block_e)).reshape(P, D)
    yb = yb * gate_rows[:, None].astype(yb.dtype)
    return jnp.zeros((T + 1, D), yb.dtype).at[tok].add(yb)[:T]


def decoder_layer(x, c, pos, pool_prev, C0, n0, m0, w_ada, b_ada, g_norm1, w_in, b_in, w_pool_grp, pool_scale,
                  w_proj_a, g_hnorm, w_proj_b, w_out, g_norm2, w_router, b_router, w_up, b_up, w_down, b_down):
    B, T, _ = x.shape
    mod = jax.nn.silu(c) @ w_ada + b_ada
    sh1, sc1, gt1, sh2, sc2, gt2 = [mm[:, None, :] for mm in jnp.split(mod, N_MOD, axis=-1)]
    h = rms_norm(x, g_norm1) * (1 + sc1) + sh1
    z = h @ w_in + b_in
    u_a, q, k, v, o, ig, fg, ga, gb = jnp.split(z, _split_points(), axis=-1)
    a, pool_new = pool_mixer(u_a, pool_prev, pos, w_pool_grp, pool_scale)
    q = q.reshape(B, T, M_HEADS, M_DK) * (M_DK ** -0.5)
    k = k.reshape(B, T, M_HEADS, M_DK)
    v = v.reshape(B, T, M_HEADS, M_DV)
    ig = GATE_CAP * jnp.tanh(ig.astype(jnp.float32) / GATE_CAP)
    lf = jax.nn.log_sigmoid(GATE_CAP * jnp.tanh(fg.astype(jnp.float32) / GATE_CAP))
    hm, C, n, m = mlstm_scan(q, k, v, ig, lf, C0, n0, m0)
    hm = hm * lax.rsqrt(jnp.mean(hm * hm, axis=-1, keepdims=True) + EPS) * g_hnorm.reshape(M_HEADS, M_DV).astype(jnp.float32)
    bm = (hm.reshape(B, T, M_V) * jax.nn.sigmoid(o.astype(jnp.float32))).astype(x.dtype)
    merged = jax.nn.sigmoid(ga) * (a @ w_proj_a) + jax.nn.sigmoid(gb) * (bm @ w_proj_b)
    x = x + gt1 * (merged @ w_out)
    h2 = rms_norm(x, g_norm2) * (1 + sc2) + sh2
    y2 = moe(h2.reshape(B * T, -1), w_router, b_router, w_up, b_up, w_down, b_down).reshape(B, T, -1)
    x = x + gt2 * y2
    return x, pool_new, C, n, m


def _normal(key, shape, scale):
    return scale * jax.random.normal(key, shape, jnp.float32)


def setup_inputs(seed: int = 0) -> dict:
    key = jax.random.key(seed)
    ks = jax.random.split(key, 32)
    L = DEPTH
    f_lo = POOL_WIDTH + 2 * M_QK + 2 * M_V + M_HEADS
    b_in = _normal(ks[10], (L, N_IN), 0.02).at[:, f_lo:f_lo + M_HEADS].add(jnp.linspace(3.0, 6.0, M_HEADS))
    return {
        'x_prompt': _normal(ks[0], (BATCH, SEQ, D_MODEL), 1.0),
        'x_sample': _normal(ks[1], (DEC_BATCH, DEC_SEQ, D_MODEL), 1.0),
        'state_pool': _normal(ks[2], (L, DEC_BATCH, POOL_BUF, POOL_WIDTH), 1.0),
        'state_C': _normal(ks[3], (L, DEC_BATCH, M_HEADS, M_DK, M_DV), 1.0),
        'state_n': _normal(ks[4], (L, DEC_BATCH, M_HEADS, M_DK), 1.0),
        'state_m': _normal(ks[5], (L, DEC_BATCH, M_HEADS), 0.5),
        'c_prompt': _normal(ks[6], (BATCH, D_MODEL), 1.0),
        'c_sample': _normal(ks[7], (DEC_BATCH, D_MODEL), 1.0),
        'w_ada': _normal(ks[8], (L, D_MODEL, N_MOD * D_MODEL), 0.5 * D_MODEL ** -0.5),
        'b_ada': _normal(ks[9], (L, N_MOD * D_MODEL), 0.02),
        'g_norm1': 1.0 + _normal(ks[11], (L, D_MODEL), 0.02),
        'w_in': _normal(ks[12], (L, D_MODEL, N_IN), D_MODEL ** -0.5),
        'b_in': b_in,
        'w_pool_grp': _normal(ks[13], (L, N_POOL_GROUPS, POOL_GW, POOL_GW), POOL_GW ** -0.5),
        'pool_scale': 1.0 + _normal(ks[14], (L, POOL_WIDTH), 0.1),
        'w_proj_a': _normal(ks[15], (L, POOL_WIDTH, D_MODEL), POOL_WIDTH ** -0.5),
        'g_hnorm': 1.0 + _normal(ks[16], (L, M_V), 0.02),
        'w_proj_b': _normal(ks[17], (L, M_V, D_MODEL), M_V ** -0.5),
        'w_out': _normal(ks[18], (L, D_MODEL, D_MODEL), D_MODEL ** -0.5),
        'g_norm2': 1.0 + _normal(ks[19], (L, D_MODEL), 0.02),
        'w_router': _normal(ks[20], (L, D_MODEL, N_EXPERTS), D_MODEL ** -0.5),
        'b_router': _normal(ks[21], (L, N_EXPERTS), 0.01),
        'w_up': _normal(ks[22], (L, N_EXPERTS, D_MODEL, 2 * D_EXPERT), D_MODEL ** -0.5),
        'b_up': _normal(ks[23], (L, N_EXPERTS, 2 * D_EXPERT), 0.02),
        'w_down': _normal(ks[24], (L, N_EXPERTS, D_EXPERT, D_MODEL), D_EXPERT ** -0.5),
        'b_down': _normal(ks[25], (L, N_EXPERTS, D_MODEL), 0.02),
        'g_final': 1.0 + _normal(ks[26], (D_MODEL,), 0.02),
    }


def reference(x_prompt, x_sample, state_pool, state_C, state_n, state_m, c_prompt, c_sample, w_ada, b_ada, g_norm1,
              w_in, b_in, w_pool_grp, pool_scale, w_proj_a, g_hnorm, w_proj_b, w_out, g_norm2, w_router, b_router,
              w_up, b_up, w_down, b_down, g_final):
    bp, tp = x_prompt.shape[0], x_prompt.shape[1]
    ts = x_sample.shape[1]
    pos_p = jnp.arange(tp, dtype=jnp.int32)
    pos_s = PAST_LEN + jnp.arange(ts, dtype=jnp.int32)
    sdt = state_C.dtype
    pdt = state_pool.dtype
    pool0 = jnp.zeros((bp, POOL_BUF, POOL_WIDTH), x_prompt.dtype)
    C0 = jnp.zeros((bp, M_HEADS, M_DK, M_DV), jnp.float32)
    n0 = jnp.zeros((bp, M_HEADS, M_DK), jnp.float32)
    m0 = jnp.zeros((bp, M_HEADS), jnp.float32)
    xp, xs = x_prompt, x_sample
    pool_p, pool_s, C_p, C_s, n_p, n_s, m_p, m_s = [], [], [], [], [], [], [], []
    for l in range(DEPTH):
        lw = (w_ada[l], b_ada[l], g_norm1[l], w_in[l], b_in[l], w_pool_grp[l], pool_scale[l], w_proj_a[l],
              g_hnorm[l], w_proj_b[l], w_out[l], g_norm2[l], w_router[l], b_router[l], w_up[l], b_up[l],
              w_down[l], b_down[l])
        xp, p1, c1, n1, m1 = decoder_layer(xp, c_prompt, pos_p, pool0, C0, n0, m0, *lw)
        xs, p2, c2, n2, m2 = decoder_layer(xs, c_sample, pos_s, state_pool[l], state_C[l], state_n[l], state_m[l], *lw)
        pool_p.append(p1.astype(pdt))
        pool_s.append(p2.astype(pdt))
        C_p.append(c1.astype(sdt))
        C_s.append(c2.astype(sdt))
        n_p.append(n1.astype(sdt))
        n_s.append(n2.astype(sdt))
        m_p.append(m1.astype(sdt))
        m_s.append(m2.astype(sdt))
    y_prompt = rms_norm(xp, g_final)
    y_sample = rms_norm(xs, g_final)
    return (y_prompt, y_sample, jnp.stack(pool_p), jnp.stack(pool_s), jnp.stack(C_p), jnp.stack(C_s),
            jnp.stack(n_p), jnp.stack(n_s), jnp.stack(m_p), jnp.stack(m_s))
```

```python
import functools

import jax
import jax.numpy as jnp
from jax import lax
from jax.experimental import pallas as pl
from jax.experimental.pallas import tpu as pltpu

F32 = jnp.float32
BF16 = jnp.bfloat16

EPS = 1e-6
N_MOD = 6
POOL_WINDOWS = (2, 4, 8, 16)
POOL_HALO = 16
M_HEADS = 8
CHUNK = 64
GATE_CAP = 15.0
PAST_LEN = 16384
TOP_K = 4
SWIGLU_LIMIT = 7.0
SWIGLU_ALPHA = 1.702
LANES = 128
MIB = 1024 * 1024

ROW_TILE = 512
COL_TILE = 512
EXPERT_ROW_TILE = 256
EXPERT_COL_TILE = 256
GATHER_ROWS = 256
ELEMENTWISE_ROWS = 64


def _params(semantics, vmem_mib):
    return pltpu.CompilerParams(dimension_semantics=semantics, vmem_limit_bytes=vmem_mib * MIB)


def _mxu(a, b):
    return jnp.dot(a.astype(BF16), b.astype(BF16), preferred_element_type=F32)


def _ada_body(c_ref, w_ref, b_ref, o_ref):
    s = jax.nn.silu(c_ref[...])
    o_ref[...] = _mxu(s, w_ref[...]) + b_ref[...]


def _ada(c, w, b):
    mc, d = c.shape
    n = w.shape[1]
    tn = min(COL_TILE, n)
    return pl.pallas_call(
        _ada_body,
        grid=(n // tn,),
        in_specs=[pl.BlockSpec((mc, d), lambda j: (0, 0)),
                  pl.BlockSpec((d, tn), lambda j: (0, j)),
                  pl.BlockSpec((1, tn), lambda j: (0, j))],
        out_specs=pl.BlockSpec((mc, tn), lambda j: (0, j)),
        out_shape=jax.ShapeDtypeStruct((mc, n), F32),
        compiler_params=_params(("arbitrary",), 40),
        name="ada_mod",
    )(c, w, b.reshape(1, n))


def _rms_mod(x, g, sc, sh):
    y = x * lax.rsqrt(jnp.mean(x * x, axis=-1, keepdims=True) + EPS)
    return (y * g) * (1.0 + sc) + sh


def _norm1_body(xp_ref, xs_ref, scp_ref, shp_ref, scs_ref, shs_ref, g_ref, h_ref, x_ref, *, nbp):
    is_p = pl.program_id(0) < nbp
    x = jnp.where(is_p, xp_ref[...], xs_ref[...])
    sc = jnp.where(is_p, scp_ref[0], scs_ref[...])
    sh = jnp.where(is_p, shp_ref[0], shs_ref[...])
    h_ref[...] = _rms_mod(x, g_ref[...], sc, sh).astype(h_ref.dtype)
    x_ref[...] = x


def _row_specs(tm, d, nbp, tiles_per_batch):
    xp = pl.BlockSpec((tm, d), lambda i: (jnp.minimum(i, nbp - 1), 0))
    xs = pl.BlockSpec((tm, d), lambda i: (jnp.maximum(i - nbp, 0), 0))
    vp = pl.BlockSpec((1, 1, d), lambda i: (jnp.minimum(i, nbp - 1) // tiles_per_batch, 0, 0))
    return xp, xs, vp, xs


def _norm1(xp, xs, scp, shp, scs, shs, g, tm, t_prompt):
    mp, d = xp.shape
    ms = xs.shape[0]
    nbp, nbs = mp // tm, ms // tm
    sp_x, ss_x, sp_v, ss_v = _row_specs(tm, d, nbp, t_prompt // tm)
    row = pl.BlockSpec((tm, d), lambda i: (i, 0))
    return pl.pallas_call(
        functools.partial(_norm1_body, nbp=nbp),
        grid=(nbp + nbs,),
        in_specs=[sp_x, ss_x, sp_v, sp_v, ss_v, ss_v, pl.BlockSpec((1, d), lambda i: (0, 0))],
        out_specs=[row, row],
        out_shape=[jax.ShapeDtypeStruct((mp + ms, d), BF16), jax.ShapeDtypeStruct((mp + ms, d), F32)],
        compiler_params=_params(("arbitrary",), 56),
        name="norm1_mod",
    )(xp, xs, scp, shp, scs, shs, g.reshape(1, d))


def _norm2_body(x_ref, scp_ref, shp_ref, scs_ref, shs_ref, g_ref, wr_ref, br_ref,
                h_ref, idx_ref, gate_ref, *, nbp):
    is_p = pl.program_id(0) < nbp
    sc = jnp.where(is_p, scp_ref[0], scs_ref[...])
    sh = jnp.where(is_p, shp_ref[0], shs_ref[...])
    h = _rms_mod(x_ref[...], g_ref[...], sc, sh)
    h_ref[...] = h
    logits = _mxu(h, wr_ref[...]) + br_ref[...]
    lane = lax.broadcasted_iota(jnp.int32, logits.shape, 1)
    vals, idxs = [], []
    for _ in range(TOP_K):
        v = jnp.max(logits, axis=-1, keepdims=True)
        ix = jnp.min(jnp.where(logits == v, lane, LANES), axis=-1, keepdims=True)
        vals.append(v)
        idxs.append(ix)
        logits = jnp.where(lane == ix, -jnp.inf, logits)
    exps = [jnp.exp(v - vals[0]) for v in vals]
    tot = exps[0] + exps[1] + exps[2] + exps[3]
    idx_out = jnp.zeros(lane.shape, jnp.int32)
    gate_out = jnp.zeros(lane.shape, F32)
    for k in range(TOP_K):
        idx_out = jnp.where(lane == k, idxs[k], idx_out)
        gate_out = jnp.where(lane == k, exps[k] / tot, gate_out)
    idx_ref[...] = idx_out
    gate_ref[...] = gate_out


def _norm2_router(x, scp, shp, scs, shs, g, w_router, b_router, tm, nbp, t_prompt):
    m, d = x.shape
    e = w_router.shape[1]
    wr = jnp.zeros((d, LANES), F32).at[:, :e].set(w_router)
    br = jnp.full((1, LANES), -jnp.inf, F32).at[0, :e].set(b_router)
    _, _, sp_v, ss_v = _row_specs(tm, d, nbp, t_prompt // tm)
    row = pl.BlockSpec((tm, d), lambda i: (i, 0))
    small = pl.BlockSpec((tm, LANES), lambda i: (i, 0))
    return pl.pallas_call(
        functools.partial(_norm2_body, nbp=nbp),
        grid=(m // tm,),
        in_specs=[row, sp_v, sp_v, ss_v, ss_v, pl.BlockSpec((1, d), lambda i: (0, 0)),
                  pl.BlockSpec((d, LANES), lambda i: (0, 0)), pl.BlockSpec((1, LANES), lambda i: (0, 0))],
        out_specs=[row, small, small],
        out_shape=[jax.ShapeDtypeStruct((m, d), F32), jax.ShapeDtypeStruct((m, LANES), jnp.int32),
                   jax.ShapeDtypeStruct((m, LANES), F32)],
        compiler_params=_params(("arbitrary",), 56),
        name="norm2_router",
    )(x, scp, shp, scs, shs, g.reshape(1, d), wr, br)


def _linear_body(a_ref, w_ref, b_ref, o_ref, wb_ref):
    @pl.when(pl.program_id(1) == 0)
    def _():
        wb_ref[...] = w_ref[...].astype(BF16)

    acc = jnp.dot(a_ref[...], wb_ref[...], preferred_element_type=F32)
    o_ref[...] = (acc + b_ref[...]).astype(o_ref.dtype)


def _linear(a, w, b, col0, n, out_dtype, tm):
    m, k = a.shape
    tn = min(COL_TILE, n)
    jb = col0 // tn
    return pl.pallas_call(
        _linear_body,
        grid=(n // tn, m // tm),
        in_specs=[pl.BlockSpec((tm, k), lambda j, i: (i, 0)),
                  pl.BlockSpec((k, tn), lambda j, i: (0, jb + j)),
                  pl.BlockSpec((1, tn), lambda j, i: (0, jb + j))],
        out_specs=pl.BlockSpec((tm, tn), lambda j, i: (i, j)),
        out_shape=jax.ShapeDtypeStruct((m, n), out_dtype),
        scratch_shapes=[pltpu.VMEM((k, tn), BF16)],
        compiler_params=_params(("arbitrary", "arbitrary"), 48),
        name="linear",
    )(a, w, b.reshape(1, -1))


def _pool_prompt_body(u_ref, w_ref, sc_ref, a_ref, buf_ref, *, tm, gw):
    t = pl.program_id(1)

    @pl.when(t == 0)
    def _():
        buf_ref[0:POOL_HALO, :] = jnp.zeros((POOL_HALO, buf_ref.shape[1]), F32)

    @pl.when(t > 0)
    def _():
        buf_ref[0:POOL_HALO, :] = buf_ref[tm:tm + POOL_HALO, :]

    buf_ref[POOL_HALO:POOL_HALO + tm, :] = u_ref[...]
    pos = t * tm + lax.broadcasted_iota(jnp.int32, (tm, 1), 0)
    for g, w in enumerate(POOL_WINDOWS):
        lo, hi = g * gw, (g + 1) * gw
        cur = buf_ref[POOL_HALO:POOL_HALO + tm, lo:hi]
        s = cur
        for j in range(1, w):
            s = s + buf_ref[POOL_HALO - j:POOL_HALO - j + tm, lo:hi]
        cnt = jnp.minimum(pos + 1, w).astype(F32)
        d = s / cnt - cur
        a = _mxu(d, w_ref[g]) * sc_ref[:, lo:hi]
        a_ref[:, lo:hi] = a.astype(a_ref.dtype)


def _pool_prompt(u, w_grp, scale, n_batch, t_len, tm):
    width = u.shape[1]
    ng, gw, _ = w_grp.shape
    tpb = t_len // tm
    return pl.pallas_call(
        functools.partial(_pool_prompt_body, tm=tm, gw=gw),
        grid=(n_batch, tpb),
        in_specs=[pl.BlockSpec((tm, width), lambda b, t: (b * tpb + t, 0)),
                  pl.BlockSpec((ng, gw, gw), lambda b, t: (0, 0, 0)),
                  pl.BlockSpec((1, width), lambda b, t: (0, 0))],
        out_specs=pl.BlockSpec((tm, width), lambda b, t: (b * tpb + t, 0)),
        out_shape=jax.ShapeDtypeStruct((n_batch * t_len, width), BF16),
        scratch_shapes=[pltpu.VMEM((tm + POOL_HALO, width), F32)],
        compiler_params=_params(("arbitrary", "arbitrary"), 40),
        name="pool_prompt",
    )(u, w_grp, scale.reshape(1, width))


def _pool_sample_body(p_ref, w_ref, sc_ref, a_ref, *, t_len, n_prev):
    g = pl.program_id(0)
    for t in range(t_len):
        cur = p_ref[n_prev + t]
        s = cur
        j = 1
        d = None
        for gi, w in enumerate(POOL_WINDOWS):
            while j < w:
                s = s + p_ref[n_prev + t - j]
                j += 1
            d_g = s / float(min(PAST_LEN + t + 1, w)) - cur
            d = d_g if d is None else jnp.where(g == gi, d_g, d)
        a = _mxu(d, w_ref[0]) * sc_ref[...]
        a_ref[t] = a.astype(a_ref.dtype)


def _pool_sample(padded_tm, w_grp, scale, t_len):
    rows, bs, width = padded_tm.shape
    ng, gw, _ = w_grp.shape
    return pl.pallas_call(
        functools.partial(_pool_sample_body, t_len=t_len, n_prev=rows - t_len),
        grid=(ng,),
        in_specs=[pl.BlockSpec((rows, bs, gw), lambda g: (0, 0, g)),
                  pl.BlockSpec((1, gw, gw), lambda g: (g, 0, 0)),
                  pl.BlockSpec((1, gw), lambda g: (0, g))],
        out_specs=pl.BlockSpec((t_len, bs, gw), lambda g: (0, 0, g)),
        out_shape=jax.ShapeDtypeStruct((t_len, bs, width), BF16),
        compiler_params=_params(("arbitrary",), 40),
        name="pool_sample",
    )(padded_tm, w_grp, scale.reshape(1, width))


def _mlstm_body(*refs, L, dk, dv, has_init, nc):
    if has_init:
        (q_ref, k_ref, v_ref, o_ref, if_ref, g_ref, c0_ref, n0_ref, m0_ref,
         bm_ref, c_out, n_out, m_out, c_scr, n_scr, m_scr) = refs
    else:
        (q_ref, k_ref, v_ref, o_ref, if_ref, g_ref,
         bm_ref, c_out, n_out, m_out, c_scr, n_scr, m_scr) = refs
    c = pl.program_id(1)

    def rows(ref):
        return ref[0] if len(ref.shape) == 3 else ref[...]

    @pl.when(c == 0)
    def _():
        if has_init:
            c_scr[...] = c0_ref[0]
            n_scr[...] = n0_ref[0]
            m_scr[...] = m0_ref[0]
        else:
            c_scr[...] = jnp.zeros(c_scr.shape, F32)
            n_scr[...] = jnp.zeros(n_scr.shape, F32)
            m_scr[...] = jnp.zeros(m_scr.shape, F32)

    q_all, k_all, v_all, o_all = rows(q_ref), rows(k_ref), rows(v_ref), rows(o_ref)
    pre = rows(if_ref)
    capped = GATE_CAP * jnp.tanh(pre / GATE_CAP)
    col = lax.broadcasted_iota(jnp.int32, pre.shape, 1)
    gates = jnp.where(col < M_HEADS, capped, jax.nn.log_sigmoid(capped))

    ti = lax.broadcasted_iota(jnp.int32, (L, L), 0)
    si = lax.broadcasted_iota(jnp.int32, (L, L), 1)
    scale = dk ** -0.5
    for h in range(M_HEADS):
        i_col = gates[:, h:h + 1]
        f_col = gates[:, M_HEADS + h:M_HEADS + h + 1]
        m_old = m_scr[:, h:h + 1]
        b_row = jnp.sum(jnp.where(ti <= si, f_col, 0.0), axis=0, keepdims=True)
        i_row = jnp.sum(jnp.where(ti == si, i_col, 0.0), axis=0, keepdims=True)
        b_col = jnp.sum(jnp.where(ti == si, b_row, 0.0), axis=1, keepdims=True)
        dmat = jnp.where(si <= ti, b_col - b_row + i_row, -jnp.inf)
        inter = b_col + m_old
        mt = jnp.maximum(inter, jnp.max(dmat, axis=1, keepdims=True))
        q = q_all[:, h * dk:(h + 1) * dk]
        k = k_all[:, h * dk:(h + 1) * dk]
        v = v_all[:, h * dv:(h + 1) * dv]
        qk = lax.dot_general(q.astype(BF16), k.astype(BF16), (((1,), (1,)), ((), ())),
                             preferred_element_type=F32) * scale
        s = qk * jnp.exp(dmat - mt)
        w_inter = jnp.exp(inter - mt)
        c_h = c_scr[h]
        num = w_inter * (_mxu(q, c_h) * scale) + _mxu(s, v)
        qn = jnp.sum(q.astype(F32) * n_scr[h:h + 1, :], axis=1, keepdims=True) * scale
        den = w_inter * qn + jnp.sum(s, axis=1, keepdims=True)
        hh = num / jnp.maximum(jnp.abs(den), jnp.exp(-mt))
        hn = hh * lax.rsqrt(jnp.mean(hh * hh, axis=1, keepdims=True) + EPS) * g_ref[:, h * dv:(h + 1) * dv]
        bm = hn * jax.nn.sigmoid(o_all[:, h * dv:(h + 1) * dv])
        if len(bm_ref.shape) == 3:
            bm_ref[0, :, h * dv:(h + 1) * dv] = bm.astype(bm_ref.dtype)
        else:
            bm_ref[:, h * dv:(h + 1) * dv] = bm.astype(bm_ref.dtype)
        bl = b_row[:, L - 1:L]
        wlog = bl - b_col + i_col
        m_new = jnp.maximum(bl + m_old, jnp.max(wlog, axis=0, keepdims=True))
        decay = jnp.exp(bl + m_old - m_new)
        kw = k.astype(F32) * jnp.exp(wlog - m_new)
        upd = lax.dot_general(kw.astype(BF16), v.astype(BF16), (((0,), (0,)), ((), ())),
                              preferred_element_type=F32)
        c_scr[h] = decay * c_h + upd
        n_scr[h:h + 1, :] = decay * n_scr[h:h + 1, :] + jnp.sum(kw, axis=0, keepdims=True)
        m_scr[:, h:h + 1] = m_new

    @pl.when(c == nc - 1)
    def _():
        c_out[0] = c_scr[...]
        n_out[0] = n_scr[...]
        m_out[0] = m_scr[...]


def _mlstm(q, k, v, o, pre_if, g_hnorm, n_batch, t_len, state=None):
    has_init = state is not None
    dk = q.shape[-1] // M_HEADS
    dv = v.shape[-1] // M_HEADS
    L = CHUNK if t_len % CHUNK == 0 else t_len
    nc = t_len // L

    if has_init:
        def tok(width):
            return pl.BlockSpec((1, L, width), lambda b, c: (b, 0, 0))
        bm_shape = (n_batch, t_len, M_HEADS * dv)
    else:
        def tok(width):
            return pl.BlockSpec((L, width), lambda b, c: (b * nc + c, 0))
        bm_shape = (n_batch * t_len, M_HEADS * dv)

    in_specs = [tok(M_HEADS * dk), tok(M_HEADS * dk), tok(M_HEADS * dv), tok(M_HEADS * dv), tok(LANES),
                pl.BlockSpec((1, M_HEADS * dv), lambda b, c: (0, 0))]
    args = [q, k, v, o, pre_if, g_hnorm.reshape(1, -1)]
    c_spec = pl.BlockSpec((1, M_HEADS, dk, dv), lambda b, c: (b, 0, 0, 0))
    n_spec = pl.BlockSpec((1, M_HEADS, dk), lambda b, c: (b, 0, 0))
    m_spec = pl.BlockSpec((1, 1, M_HEADS), lambda b, c: (b, 0, 0))
    if has_init:
        c0, n0, m0 = state
        in_specs += [c_spec, n_spec, m_spec]
        args += [c0, n0, m0.reshape(n_batch, 1, M_HEADS)]
    bm, c_new, n_new, m_new = pl.pallas_call(
        functools.partial(_mlstm_body, L=L, dk=dk, dv=dv, has_init=has_init, nc=nc),
        grid=(n_batch, nc),
        in_specs=in_specs,
        out_specs=[tok(M_HEADS * dv), c_spec, n_spec, m_spec],
        out_shape=[jax.ShapeDtypeStruct(bm_shape, BF16),
                   jax.ShapeDtypeStruct((n_batch, M_HEADS, dk, dv), F32),
                   jax.ShapeDtypeStruct((n_batch, M_HEADS, dk), F32),
                   jax.ShapeDtypeStruct((n_batch, 1, M_HEADS), F32)],
        scratch_shapes=[pltpu.VMEM((M_HEADS, dk, dv), F32), pltpu.VMEM((M_HEADS, dk), F32),
                        pltpu.VMEM((1, M_HEADS), F32)],
        compiler_params=_params(("arbitrary", "arbitrary"), 48),
        name="mlstm_state" if has_init else "mlstm_prompt",
    )(*args)
    return bm, c_new, n_new, m_new.reshape(n_batch, M_HEADS)


def _merge_body(a_ref, bm_ref, wa_ref, wb_ref, ga_ref, gb_ref, o_ref, wab_ref, wbb_ref):
    @pl.when(pl.program_id(1) == 0)
    def _():
        wab_ref[...] = wa_ref[...].astype(BF16)
        wbb_ref[...] = wb_ref[...].astype(BF16)

    pa = jnp.dot(a_ref[...], wab_ref[...], preferred_element_type=F32)
    pb = jnp.dot(bm_ref[...], wbb_ref[...], preferred_element_type=F32)
    merged = jax.nn.sigmoid(ga_ref[...]) * pa + jax.nn.sigmoid(gb_ref[...]) * pb
    o_ref[...] = merged.astype(o_ref.dtype)


def _merge(a, bm, w_a, w_b, zg, tm):
    m, ka = a.shape
    kb = bm.shape[1]
    d = w_a.shape[1]
    tn = min(COL_TILE, d)
    nj = d // tn
    return pl.pallas_call(
        _merge_body,
        grid=(nj, m // tm),
        in_specs=[pl.BlockSpec((tm, ka), lambda j, i: (i, 0)),
                  pl.BlockSpec((tm, kb), lambda j, i: (i, 0)),
                  pl.BlockSpec((ka, tn), lambda j, i: (0, j)),
                  pl.BlockSpec((kb, tn), lambda j, i: (0, j)),
                  pl.BlockSpec((tm, tn), lambda j, i: (i, j)),
                  pl.BlockSpec((tm, tn), lambda j, i: (i, nj + j))],
        out_specs=pl.BlockSpec((tm, tn), lambda j, i: (i, j)),
        out_shape=jax.ShapeDtypeStruct((m, d), BF16),
        scratch_shapes=[pltpu.VMEM((ka, tn), BF16), pltpu.VMEM((kb, tn), BF16)],
        compiler_params=_params(("arbitrary", "arbitrary"), 56),
        name="merge",
    )(a, bm, w_a, w_b, zg, zg)


def _outproj_body(a_ref, w_ref, x_ref, gp_ref, gs_ref, o_ref, wb_ref, *, nbp):
    @pl.when(pl.program_id(1) == 0)
    def _():
        wb_ref[...] = w_ref[...].astype(BF16)

    gt = jnp.where(pl.program_id(1) < nbp, gp_ref[0], gs_ref[...])
    o_ref[...] = x_ref[...] + gt * jnp.dot(a_ref[...], wb_ref[...], preferred_element_type=F32)


def _outproj(a, w, x, gt_p, gt_s, tm, nbp, t_prompt):
    m, k = a.shape
    d = w.shape[1]
    tn = min(COL_TILE, d)
    tpb = t_prompt // tm
    return pl.pallas_call(
        functools.partial(_outproj_body, nbp=nbp),
        grid=(d // tn, m // tm),
        in_specs=[pl.BlockSpec((tm, k), lambda j, i: (i, 0)),
                  pl.BlockSpec((k, tn), lambda j, i: (0, j)),
                  pl.BlockSpec((tm, tn), lambda j, i: (i, j)),
                  pl.BlockSpec((1, 1, tn), lambda j, i: (jnp.minimum(i, nbp - 1) // tpb, 0, j)),
                  pl.BlockSpec((tm, tn), lambda j, i: (jnp.maximum(i - nbp, 0), j))],
        out_specs=pl.BlockSpec((tm, tn), lambda j, i: (i, j)),
        out_shape=jax.ShapeDtypeStruct((m, d), F32),
        scratch_shapes=[pltpu.VMEM((k, tn), BF16)],
        compiler_params=_params(("arbitrary", "arbitrary"), 48),
        name="outproj",
    )(a, w, x, gt_p, gt_s)


def _gather_body(tok_ref, h_hbm, o_ref, buf_ref, sem):
    n = buf_ref.shape[0]

    def copy(r):
        return pltpu.make_async_copy(h_hbm.at[pl.ds(tok_ref[0, 0, r], 1)], buf_ref.at[pl.ds(r, 1)], sem)

    def start(r, carry):
        copy(r).start()
        return carry

    def wait(r, carry):
        copy(r).wait()
        return carry

    lax.fori_loop(0, n, start, 0)
    lax.fori_loop(0, n, wait, 0)
    o_ref[...] = buf_ref[...].astype(o_ref.dtype)


def _gather_rows(h, tok_slot, rows):
    p = tok_slot.shape[0]
    d = h.shape[1]
    steps = p // rows
    return pl.pallas_call(
        _gather_body,
        grid=(steps,),
        in_specs=[pl.BlockSpec((1, 1, rows), lambda i: (i, 0, 0), memory_space=pltpu.SMEM),
                  pl.BlockSpec(memory_space=pl.ANY)],
        out_specs=pl.BlockSpec((rows, d), lambda i: (i, 0)),
        out_shape=jax.ShapeDtypeStruct((p, d), BF16),
        scratch_shapes=[pltpu.VMEM((rows, d), F32), pltpu.SemaphoreType.DMA],
        compiler_params=_params(("arbitrary",), 32),
        name="moe_gather",
    )(tok_slot.reshape(steps, 1, rows), h)


def _first_tile_of_expert(te_ref, i):
    return jnp.logical_or(i == 0, te_ref[i] != te_ref[jnp.maximum(i - 1, 0)])


def _expert_up_body(te_ref, nu_ref, x_ref, wg_ref, wl_ref, bg_ref, bl_ref, o_ref, wgb_ref, wlb_ref):
    i = pl.program_id(1)
    valid = i < nu_ref[0]

    @pl.when(jnp.logical_and(valid, _first_tile_of_expert(te_ref, i)))
    def _():
        wgb_ref[...] = wg_ref[0].astype(BF16)
        wlb_ref[...] = wl_ref[0].astype(BF16)

    @pl.when(valid)
    def _():
        x = x_ref[...]
        up_g = jnp.dot(x, wgb_ref[...], preferred_element_type=F32) + bg_ref[0]
        up_l = jnp.dot(x, wlb_ref[...], preferred_element_type=F32) + bl_ref[0]
        glu = jnp.minimum(up_g, SWIGLU_LIMIT)
        lin = jnp.clip(up_l, -SWIGLU_LIMIT, SWIGLU_LIMIT)
        act = glu * jax.nn.sigmoid(SWIGLU_ALPHA * glu) * (lin + 1.0)
        o_ref[...] = act.astype(o_ref.dtype)

    @pl.when(jnp.logical_not(valid))
    def _():
        o_ref[...] = jnp.zeros(o_ref.shape, o_ref.dtype)


def _expert_up(xs, w_up, b_up, tile_e, n_used, tm):
    p, d = xs.shape
    ne, _, two_de = w_up.shape
    de = two_de // 2
    tn = min(EXPERT_COL_TILE, de)
    nj = de // tn

    def tile(i, nu):
        return jnp.minimum(i, nu[0] - 1)

    grid_spec = pltpu.PrefetchScalarGridSpec(
        num_scalar_prefetch=2,
        grid=(nj, p // tm),
        in_specs=[pl.BlockSpec((tm, d), lambda j, i, te, nu: (tile(i, nu), 0)),
                  pl.BlockSpec((1, d, tn), lambda j, i, te, nu: (te[tile(i, nu)], 0, j)),
                  pl.BlockSpec((1, d, tn), lambda j, i, te, nu: (te[tile(i, nu)], 0, nj + j)),
                  pl.BlockSpec((1, 1, tn), lambda j, i, te, nu: (te[tile(i, nu)], 0, j)),
                  pl.BlockSpec((1, 1, tn), lambda j, i, te, nu: (te[tile(i, nu)], 0, nj + j))],
        out_specs=pl.BlockSpec((tm, tn), lambda j, i, te, nu: (i, j)),
        scratch_shapes=[pltpu.VMEM((d, tn), BF16), pltpu.VMEM((d, tn), BF16)],
    )
    b3 = b_up.reshape(ne, 1, two_de)
    return pl.pallas_call(
        _expert_up_body,
        grid_spec=grid_spec,
        out_shape=jax.ShapeDtypeStruct((p, de), BF16),
        compiler_params=_params(("arbitrary", "arbitrary"), 48),
        name="expert_up",
    )(tile_e, n_used, xs, w_up, w_up, b3, b3)


def _expert_down_body(te_ref, nu_ref, a_ref, w_ref, b_ref, o_ref, wb_ref):
    i = pl.program_id(1)
    valid = i < nu_ref[0]

    @pl.when(jnp.logical_and(valid, _first_tile_of_expert(te_ref, i)))
    def _():
        wb_ref[...] = w_ref[0].astype(BF16)

    @pl.when(valid)
    def _():
        o_ref[...] = jnp.dot(a_ref[...], wb_ref[...], preferred_element_type=F32) + b_ref[0]

    @pl.when(jnp.logical_not(valid))
    def _():
        o_ref[...] = jnp.zeros(o_ref.shape, o_ref.dtype)


def _expert_down(act, w_down, b_down, tile_e, n_used, tm):
    p, de = act.shape
    ne, _, d = w_down.shape
    tn = min(COL_TILE, d)

    def tile(i, nu):
        return jnp.minimum(i, nu[0] - 1)

    grid_spec = pltpu.PrefetchScalarGridSpec(
        num_scalar_prefetch=2,
        grid=(d // tn, p // tm),
        in_specs=[pl.BlockSpec((tm, de), lambda j, i, te, nu: (tile(i, nu), 0)),
                  pl.BlockSpec((1, de, tn), lambda j, i, te, nu: (te[tile(i, nu)], 0, j)),
                  pl.BlockSpec((1, 1, tn), lambda j, i, te, nu: (te[tile(i, nu)], 0, j))],
        out_specs=pl.BlockSpec((tm, tn), lambda j, i, te, nu: (i, j)),
        scratch_shapes=[pltpu.VMEM((de, tn), BF16)],
    )
    return pl.pallas_call(
        _expert_down_body,
        grid_spec=grid_spec,
        out_shape=jax.ShapeDtypeStruct((p, d), F32),
        compiler_params=_params(("arbitrary", "arbitrary"), 48),
        name="expert_down",
    )(tile_e, n_used, act, w_down, b_down.reshape(ne, 1, d))


def _combine_body(pos_ref, y_hbm, x_ref, gate_ref, gp_ref, gs_ref, gf_ref, o_ref, buf_ref, sem, *, nbp):
    tmc = x_ref.shape[0]
    n = TOP_K * tmc

    def copy(r):
        return pltpu.make_async_copy(y_hbm.at[pl.ds(pos_ref[0, 0, r], 1)], buf_ref.at[pl.ds(r, 1)], sem)

    def start(r, carry):
        copy(r).start()
        return carry

    def wait(r, carry):
        copy(r).wait()
        return carry

    lax.fori_loop(0, n, start, 0)
    lax.fori_loop(0, n, wait, 0)
    gates = gate_ref[...]
    y = gates[:, 0:1] * buf_ref[0:tmc, :]
    for k in range(1, TOP_K):
        y = y + gates[:, k:k + 1] * buf_ref[k * tmc:(k + 1) * tmc, :]
    gt = jnp.where(pl.program_id(0) < nbp, gp_ref[0], gs_ref[...])
    x = x_ref[...] + gt * y
    o_ref[...] = x * lax.rsqrt(jnp.mean(x * x, axis=-1, keepdims=True) + EPS) * gf_ref[...]


def _combine_final(y_sorted, pos, x, gates, gt_p, gt_s, g_final, tmc, nbp, t_prompt):
    m, d = x.shape
    steps = m // tmc
    tpb = t_prompt // tmc
    pos_blk = pos.reshape(steps, tmc, TOP_K).transpose(0, 2, 1).reshape(steps, 1, TOP_K * tmc)
    return pl.pallas_call(
        functools.partial(_combine_body, nbp=nbp),
        grid=(steps,),
        in_specs=[pl.BlockSpec((1, 1, TOP_K * tmc), lambda i: (i, 0, 0), memory_space=pltpu.SMEM),
                  pl.BlockSpec(memory_space=pl.ANY),
                  pl.BlockSpec((tmc, d), lambda i: (i, 0)),
                  pl.BlockSpec((tmc, LANES), lambda i: (i, 0)),
                  pl.BlockSpec((1, 1, d), lambda i: (jnp.minimum(i, nbp - 1) // tpb, 0, 0)),
                  pl.BlockSpec((tmc, d), lambda i: (jnp.maximum(i - nbp, 0), 0)),
                  pl.BlockSpec((1, d), lambda i: (0, 0))],
        out_specs=pl.BlockSpec((tmc, d), lambda i: (i, 0)),
        out_shape=jax.ShapeDtypeStruct((m, d), F32),
        scratch_shapes=[pltpu.VMEM((TOP_K * tmc, d), F32), pltpu.SemaphoreType.DMA],
        compiler_params=_params(("arbitrary",), 32),
        name="moe_combine_norm",
    )(pos_blk, y_sorted, x, gates, gt_p, gt_s, g_final.reshape(1, d))


def _route(top_idx, n_experts, tm):
    t = top_idx.shape[0]
    a = t * TOP_K
    flat_e = top_idx.reshape(a)
    order = jnp.argsort(flat_e, stable=True).astype(jnp.int32)
    sorted_e = flat_e[order]
    counts = jnp.zeros((n_experts,), jnp.int32).at[flat_e].add(1)
    padded = ((counts + tm - 1) // tm) * tm
    pad_end = jnp.cumsum(padded)
    pad_start = pad_end - padded
    start = jnp.cumsum(counts) - counts
    dest = pad_start[sorted_e] + (jnp.arange(a, dtype=jnp.int32) - start[sorted_e])
    n_tiles = -(-a // tm) + n_experts
    tok_slot = jnp.zeros((n_tiles * tm,), jnp.int32).at[dest].set(order // TOP_K)
    pos = jnp.zeros((a,), jnp.int32).at[order].set(dest)
    n_used = (pad_end[-1] // tm).astype(jnp.int32)
    tile_ids = jnp.minimum(jnp.arange(n_tiles, dtype=jnp.int32), n_used - 1)
    tile_e = jnp.minimum(jnp.searchsorted(pad_end, tile_ids * tm, side="right"), n_experts - 1).astype(jnp.int32)
    return tok_slot, pos, tile_e, n_used.reshape(1)


def _tile_rows(*lengths):
    tm = ROW_TILE
    while any(n % tm for n in lengths):
        tm //= 2
    return tm


def kernel(x_prompt, x_sample, state_pool, state_C, state_n, state_m, c_prompt, c_sample, w_ada, b_ada, g_norm1,
           w_in, b_in, w_pool_grp, pool_scale, w_proj_a, g_hnorm, w_proj_b, w_out, g_norm2, w_router, b_router,
           w_up, b_up, w_down, b_down, g_final):
    bp, tp, d = x_prompt.shape
    bs, ts, _ = x_sample.shape
    assert w_ada.shape[0] == 1, "one decoder layer is supported"
    sdt, pdt = state_C.dtype, state_pool.dtype
    (state_pool, state_C, state_n, state_m, w_ada, b_ada, g_norm1, w_in, b_in, w_pool_grp, pool_scale, w_proj_a,
     g_hnorm, w_proj_b, w_out, g_norm2, w_router, b_router, w_up, b_up, w_down, b_down) = [
        p.reshape(p.shape[1:]) for p in
        (state_pool, state_C, state_n, state_m, w_ada, b_ada, g_norm1, w_in, b_in, w_pool_grp, pool_scale,
         w_proj_a, g_hnorm, w_proj_b, w_out, g_norm2, w_router, b_router, w_up, b_up, w_down, b_down)]
    mp, ms = bp * tp, bs * ts
    tm = _tile_rows(tp, ms)
    tmc = min(ELEMENTWISE_ROWS, tm)
    nbp = mp // tm
    pool_w = w_pool_grp.shape[0] * w_pool_grp.shape[1]
    n_prev = state_pool.shape[1]
    dv = d // M_HEADS
    qk_w = w_in.shape[1] - pool_w - 2 * M_HEADS * dv - 2 * M_HEADS - 2 * d
    qk_w //= 2
    offs = {"u": 0, "q": pool_w, "k": pool_w + qk_w, "v": pool_w + 2 * qk_w}
    offs["o"] = offs["v"] + M_HEADS * dv
    offs["if"] = offs["o"] + M_HEADS * dv
    offs["gab"] = offs["if"] + 2 * M_HEADS

    c_all = jnp.concatenate([c_prompt, c_sample], axis=0)
    n_c = c_all.shape[0]
    c_all = jnp.pad(c_all, ((0, (-n_c) % 8), (0, 0)))
    mod = _ada(c_all, w_ada, b_ada)

    def mod_part(kind):
        part = mod[:, kind * d:(kind + 1) * d]
        return part[:bp].reshape(bp, 1, d), jnp.repeat(part[bp:n_c], ts, axis=0)

    (sh1_p, sh1_s), (sc1_p, sc1_s), (gt1_p, gt1_s) = mod_part(0), mod_part(1), mod_part(2)
    (sh2_p, sh2_s), (sc2_p, sc2_s), (gt2_p, gt2_s) = mod_part(3), mod_part(4), mod_part(5)

    h1, x_all = _norm1(x_prompt.reshape(mp, d), x_sample.reshape(ms, d), sc1_p, sh1_p, sc1_s, sh1_s,
                       g_norm1, tmc, tp)

    w_in_l, b_in_l = w_in, b_in
    u = _linear(h1, w_in_l, b_in_l, offs["u"], pool_w, F32, tm)
    q = _linear(h1, w_in_l, b_in_l, offs["q"], qk_w, BF16, tm)
    k = _linear(h1, w_in_l, b_in_l, offs["k"], qk_w, F32, tm)
    v = _linear(h1, w_in_l, b_in_l, offs["v"], M_HEADS * dv, BF16, tm)
    o = _linear(h1, w_in_l, b_in_l, offs["o"], M_HEADS * dv, F32, tm)
    w_if = jnp.pad(w_in_l[:, offs["if"]:offs["gab"]], ((0, 0), (0, LANES - 2 * M_HEADS)))
    b_if = jnp.pad(b_in_l[offs["if"]:offs["gab"]], (0, LANES - 2 * M_HEADS))
    pre_if = _linear(h1, w_if, b_if, 0, LANES, F32, tm)
    zg = _linear(h1, w_in_l[:, offs["gab"]:], b_in_l[offs["gab"]:], 0, 2 * d, F32, tm)

    a_p = _pool_prompt(u, w_pool_grp, pool_scale, bp, tp, tm)
    u_s = u[mp:].reshape(bs, ts, pool_w)
    padded_s = jnp.concatenate([state_pool, u_s], axis=1)
    a_s = _pool_sample(padded_s.transpose(1, 0, 2), w_pool_grp, pool_scale, ts)
    a_all = jnp.concatenate([a_p, a_s.transpose(1, 0, 2).reshape(ms, pool_w)], axis=0)
    pool_p = u[:mp].reshape(bp, tp, pool_w)[:, tp - n_prev:, :]
    pool_s = padded_s[:, ts:, :]

    bm_p, c_p, n_p, m_p = _mlstm(q, k, v, o, pre_if, g_hnorm, bp, tp)

    def sample_rows(arr):
        return arr[mp:].reshape(bs, ts, arr.shape[1])

    bm_s, c_s, n_s, m_s = _mlstm(sample_rows(q), sample_rows(k), sample_rows(v), sample_rows(o),
                                 sample_rows(pre_if), g_hnorm, bs, ts,
                                 state=(state_C, state_n, state_m))
    bm_all = jnp.concatenate([bm_p, bm_s.reshape(ms, M_HEADS * dv)], axis=0)

    merged = _merge(a_all, bm_all, w_proj_a, w_proj_b, zg, tm)
    x1 = _outproj(merged, w_out, x_all, gt1_p, gt1_s, tm, nbp, tp)

    h2, top_idx, gates = _norm2_router(x1, sc2_p, sh2_p, sc2_s, sh2_s, g_norm2, w_router, b_router,
                                       tmc, mp // tmc, tp)
    n_experts = w_router.shape[1]
    te = EXPERT_ROW_TILE
    tok_slot, pos, tile_e, n_used = _route(top_idx[:, :TOP_K], n_experts, te)
    xs = _gather_rows(h2, tok_slot, min(GATHER_ROWS, te))
    act = _expert_up(xs, w_up, b_up, tile_e, n_used, te)
    y_sorted = _expert_down(act, w_down, b_down, tile_e, n_used, te)
    y_all =_combine_final(y_sorted, pos, x1, gates, gt2_p, gt2_s, g_final, tmc, mp // tmc, tp)

    return (y_all[:mp].reshape(bp, tp, d), y_all[mp:].reshape(bs, ts, d),
            pool_p.astype(pdt)[None], pool_s.astype(pdt)[None],
            c_p.astype(sdt)[None], c_s.astype(sdt)[None],
            n_p.astype(sdt)[None], n_s.astype(sdt)[None],
            m_p.astype(sdt)[None], m_s.astype(sdt)[None])
```

```python
import functools

import jax
import jax.numpy as jnp
from jax import lax
from jax.experimental import pallas as pl
from jax.experimental.pallas import tpu as pltpu

F32 = jnp.float32
BF16 = jnp.bfloat16

EPS = 1e-6
N_MOD = 6
POOL_WINDOWS = (2, 4, 8, 16)
POOL_HALO = 16
M_HEADS = 8
CHUNK = 64
GATE_CAP = 15.0
PAST_LEN = 16384
TOP_K = 4
SWIGLU_LIMIT = 7.0
SWIGLU_ALPHA = 1.702
LANES = 128
SUBLANES = 8
MIB = 1024 * 1024

ROW_TILE = 512
COL_TILE = 512
EXPERT_ROW_TILE = 256
EXPERT_COL_TILE = 256
SEGMENT_TILES = 8
ELEMENTWISE_ROWS = 64


def _params(semantics, vmem_mib):
    return pltpu.CompilerParams(dimension_semantics=semantics, vmem_limit_bytes=vmem_mib * MIB)


def _mxu(a, b):
    return jnp.dot(a.astype(BF16), b.astype(BF16), preferred_element_type=F32)


def _ada_body(c_ref, w_ref, b_ref, o_ref):
    s = jax.nn.silu(c_ref[...])
    o_ref[...] = _mxu(s, w_ref[...]) + b_ref[...]


def _ada(c, w, b):
    mc, d = c.shape
    n = w.shape[1]
    tn = min(COL_TILE, n)
    return pl.pallas_call(
        _ada_body,
        grid=(n // tn,),
        in_specs=[pl.BlockSpec((mc, d), lambda j: (0, 0)),
                  pl.BlockSpec((d, tn), lambda j: (0, j)),
                  pl.BlockSpec((1, tn), lambda j: (0, j))],
        out_specs=pl.BlockSpec((mc, tn), lambda j: (0, j)),
        out_shape=jax.ShapeDtypeStruct((mc, n), F32),
        compiler_params=_params(("arbitrary",), 40),
        name="ada_mod",
    )(c, w, b.reshape(1, n))


def _rms_mod(x, g, sc, sh):
    y = x * lax.rsqrt(jnp.mean(x * x, axis=-1, keepdims=True) + EPS)
    return (y * g) * (1.0 + sc) + sh


def _norm1_body(xp_ref, xs_ref, scp_ref, shp_ref, scs_ref, shs_ref, g_ref, h_ref, x_ref, *, nbp):
    is_p = pl.program_id(0) < nbp
    x = jnp.where(is_p, xp_ref[...], xs_ref[...])
    sc = jnp.where(is_p, scp_ref[0], scs_ref[...])
    sh = jnp.where(is_p, shp_ref[0], shs_ref[...])
    h_ref[...] = _rms_mod(x, g_ref[...], sc, sh).astype(h_ref.dtype)
    x_ref[...] = x


def _row_specs(tm, d, nbp, tiles_per_batch):
    xp = pl.BlockSpec((tm, d), lambda i: (jnp.minimum(i, nbp - 1), 0))
    xs = pl.BlockSpec((tm, d), lambda i: (jnp.maximum(i - nbp, 0), 0))
    vp = pl.BlockSpec((1, 1, d), lambda i: (jnp.minimum(i, nbp - 1) // tiles_per_batch, 0, 0))
    return xp, xs, vp, xs


def _norm1(xp, xs, scp, shp, scs, shs, g, tm, t_prompt):
    mp, d = xp.shape
    ms = xs.shape[0]
    nbp, nbs = mp // tm, ms // tm
    sp_x, ss_x, sp_v, ss_v = _row_specs(tm, d, nbp, t_prompt // tm)
    row = pl.BlockSpec((tm, d), lambda i: (i, 0))
    return pl.pallas_call(
        functools.partial(_norm1_body, nbp=nbp),
        grid=(nbp + nbs,),
        in_specs=[sp_x, ss_x, sp_v, sp_v, ss_v, ss_v, pl.BlockSpec((1, d), lambda i: (0, 0))],
        out_specs=[row, row],
        out_shape=[jax.ShapeDtypeStruct((mp + ms, d), BF16), jax.ShapeDtypeStruct((mp + ms, d), F32)],
        compiler_params=_params(("arbitrary",), 56),
        name="norm1_mod",
    )(xp, xs, scp, shp, scs, shs, g.reshape(1, d))


def _norm2_body(x_ref, scp_ref, shp_ref, scs_ref, shs_ref, g_ref, wr_ref, br_ref,
                h_ref, idx_ref, gate_ref, *, nbp):
    is_p = pl.program_id(0) < nbp
    sc = jnp.where(is_p, scp_ref[0], scs_ref[...])
    sh = jnp.where(is_p, shp_ref[0], shs_ref[...])
    h = _rms_mod(x_ref[...], g_ref[...], sc, sh)
    h_ref[...] = h
    logits = _mxu(h, wr_ref[...]) + br_ref[...]
    lane = lax.broadcasted_iota(jnp.int32, logits.shape, 1)
    vals, idxs = [], []
    for _ in range(TOP_K):
        v = jnp.max(logits, axis=-1, keepdims=True)
        ix = jnp.min(jnp.where(logits == v, lane, LANES), axis=-1, keepdims=True)
        vals.append(v)
        idxs.append(ix)
        logits = jnp.where(lane == ix, -jnp.inf, logits)
    exps = [jnp.exp(v - vals[0]) for v in vals]
    tot = exps[0] + exps[1] + exps[2] + exps[3]
    idx_out = jnp.zeros(lane.shape, jnp.int32)
    gate_out = jnp.zeros(lane.shape, F32)
    for k in range(TOP_K):
        idx_out = jnp.where(lane == k, idxs[k], idx_out)
        gate_out = jnp.where(lane == k, exps[k] / tot, gate_out)
    idx_ref[...] = idx_out
    gate_ref[...] = gate_out


def _norm2_router(x, scp, shp, scs, shs, g, w_router, b_router, tm, nbp, t_prompt):
    m, d = x.shape
    e = w_router.shape[1]
    wr = jnp.zeros((d, LANES), F32).at[:, :e].set(w_router)
    br = jnp.full((1, LANES), -jnp.inf, F32).at[0, :e].set(b_router)
    _, _, sp_v, ss_v = _row_specs(tm, d, nbp, t_prompt // tm)
    row = pl.BlockSpec((tm, d), lambda i: (i, 0))
    small = pl.BlockSpec((tm, LANES), lambda i: (i, 0))
    return pl.pallas_call(
        functools.partial(_norm2_body, nbp=nbp),
        grid=(m // tm,),
        in_specs=[row, sp_v, sp_v, ss_v, ss_v, pl.BlockSpec((1, d), lambda i: (0, 0)),
                  pl.BlockSpec((d, LANES), lambda i: (0, 0)), pl.BlockSpec((1, LANES), lambda i: (0, 0))],
        out_specs=[row, small, small],
        out_shape=[jax.ShapeDtypeStruct((m, d), F32), jax.ShapeDtypeStruct((m, LANES), jnp.int32),
                   jax.ShapeDtypeStruct((m, LANES), F32)],
        compiler_params=_params(("arbitrary",), 56),
        name="norm2_router",
    )(x, scp, shp, scs, shs, g.reshape(1, d), wr, br)


def _linear_t_body(a_ref, wt_ref, b_ref, o_ref, wb_ref):
    @pl.when(pl.program_id(1) == 0)
    def _():
        wb_ref[...] = wt_ref[...].astype(BF16)

    acc = lax.dot_general(a_ref[...], wb_ref[...], (((1,), (1,)), ((), ())), preferred_element_type=F32)
    o_ref[...] = (acc + b_ref[...]).astype(o_ref.dtype)


def _linear_t(a, wt, b, row0, n, out_dtype, tm):
    m, k = a.shape
    tn = min(COL_TILE, n)
    return pl.pallas_call(
        _linear_t_body,
        grid=(n // tn, m // tm),
        in_specs=[pl.BlockSpec((tm, k), lambda j, i: (i, 0)),
                  pl.BlockSpec((pl.Element(tn), pl.Element(k)),
                               lambda j, i: (pl.multiple_of(row0 + j * tn, SUBLANES), 0)),
                  pl.BlockSpec((1, tn), lambda j, i: (0, j))],
        out_specs=pl.BlockSpec((tm, tn), lambda j, i: (i, j)),
        out_shape=jax.ShapeDtypeStruct((m, n), out_dtype),
        scratch_shapes=[pltpu.VMEM((tn, k), BF16)],
        compiler_params=_params(("arbitrary", "arbitrary"), 48),
        name="linear_t",
    )(a, wt, b.reshape(1, -1))


def _pool_prompt_body(u_ref, w_ref, sc_ref, a_ref, buf_ref, *, tm, gw):
    t = pl.program_id(1)

    @pl.when(t == 0)
    def _():
        buf_ref[0:POOL_HALO, :] = jnp.zeros((POOL_HALO, buf_ref.shape[1]), F32)

    @pl.when(t > 0)
    def _():
        buf_ref[0:POOL_HALO, :] = buf_ref[tm:tm + POOL_HALO, :]

    buf_ref[POOL_HALO:POOL_HALO + tm, :] = u_ref[...]
    pos = t * tm + lax.broadcasted_iota(jnp.int32, (tm, 1), 0)
    for g, w in enumerate(POOL_WINDOWS):
        lo, hi = g * gw, (g + 1) * gw
        cur = buf_ref[POOL_HALO:POOL_HALO + tm, lo:hi]
        s = cur
        for j in range(1, w):
            s = s + buf_ref[POOL_HALO - j:POOL_HALO - j + tm, lo:hi]
        cnt = jnp.minimum(pos + 1, w).astype(F32)
        d = s / cnt - cur
        a = _mxu(d, w_ref[g]) * sc_ref[:, lo:hi]
        a_ref[:, lo:hi] = a.astype(a_ref.dtype)


def _pool_prompt(u, w_grp, scale, n_batch, t_len, tm):
    width = u.shape[1]
    ng, gw, _ = w_grp.shape
    tpb = t_len // tm
    return pl.pallas_call(
        functools.partial(_pool_prompt_body, tm=tm, gw=gw),
        grid=(n_batch, tpb),
        in_specs=[pl.BlockSpec((tm, width), lambda b, t: (b * tpb + t, 0)),
                  pl.BlockSpec((ng, gw, gw), lambda b, t: (0, 0, 0)),
                  pl.BlockSpec((1, width), lambda b, t: (0, 0))],
        out_specs=pl.BlockSpec((tm, width), lambda b, t: (b * tpb + t, 0)),
        out_shape=jax.ShapeDtypeStruct((n_batch * t_len, width), BF16),
        scratch_shapes=[pltpu.VMEM((tm + POOL_HALO, width), F32)],
        compiler_params=_params(("arbitrary", "arbitrary"), 40),
        name="pool_prompt",
    )(u, w_grp, scale.reshape(1, width))


def _pool_sample_body(p_ref, w_ref, sc_ref, a_ref, *, t_len, n_prev):
    g = pl.program_id(0)
    for t in range(t_len):
        cur = p_ref[n_prev + t]
        s = cur
        j = 1
        d = None
        for gi, w in enumerate(POOL_WINDOWS):
            while j < w:
                s = s + p_ref[n_prev + t - j]
                j += 1
            d_g = s / float(min(PAST_LEN + t + 1, w)) - cur
            d = d_g if d is None else jnp.where(g == gi, d_g, d)
        a = _mxu(d, w_ref[0]) * sc_ref[...]
        a_ref[t] = a.astype(a_ref.dtype)


def _pool_sample(padded_tm, w_grp, scale, t_len):
    rows, bs, width = padded_tm.shape
    ng, gw, _ = w_grp.shape
    return pl.pallas_call(
        functools.partial(_pool_sample_body, t_len=t_len, n_prev=rows - t_len),
        grid=(ng,),
        in_specs=[pl.BlockSpec((rows, bs, gw), lambda g: (0, 0, g)),
                  pl.BlockSpec((1, gw, gw), lambda g: (g, 0, 0)),
                  pl.BlockSpec((1, gw), lambda g: (0, g))],
        out_specs=pl.BlockSpec((t_len, bs, gw), lambda g: (0, 0, g)),
        out_shape=jax.ShapeDtypeStruct((t_len, bs, width), BF16),
        compiler_params=_params(("arbitrary",), 40),
        name="pool_sample",
    )(padded_tm, w_grp, scale.reshape(1, width))


def _mlstm_body(*refs, L, dk, dv, has_init, nc):
    if has_init:
        (q_ref, k_ref, v_ref, o_ref, if_ref, g_ref, c0_ref, n0_ref, m0_ref,
         bm_ref, c_out, n_out, m_out, c_scr, n_scr, m_scr) = refs
    else:
        (q_ref, k_ref, v_ref, o_ref, if_ref, g_ref,
         bm_ref, c_out, n_out, m_out, c_scr, n_scr, m_scr) = refs
    c = pl.program_id(1)

    def rows(ref):
        return ref[0] if len(ref.shape) == 3 else ref[...]

    @pl.when(c == 0)
    def _():
        if has_init:
            c_scr[...] = c0_ref[0]
            n_scr[...] = n0_ref[0]
            m_scr[...] = m0_ref[0]
        else:
            c_scr[...] = jnp.zeros(c_scr.shape, F32)
            n_scr[...] = jnp.zeros(n_scr.shape, F32)
            m_scr[...] = jnp.zeros(m_scr.shape, F32)

    q_all, k_all, v_all, o_all = rows(q_ref), rows(k_ref), rows(v_ref), rows(o_ref)
    pre = rows(if_ref)
    capped = GATE_CAP * jnp.tanh(pre / GATE_CAP)
    col = lax.broadcasted_iota(jnp.int32, pre.shape, 1)
    gates = jnp.where(col < M_HEADS, capped, jax.nn.log_sigmoid(capped))

    ti = lax.broadcasted_iota(jnp.int32, (L, L), 0)
    si = lax.broadcasted_iota(jnp.int32, (L, L), 1)
    scale = dk ** -0.5
    for h in range(M_HEADS):
        i_col = gates[:, h:h + 1]
        f_col = gates[:, M_HEADS + h:M_HEADS + h + 1]
        m_old = m_scr[:, h:h + 1]
        b_row = jnp.sum(jnp.where(ti <= si, f_col, 0.0), axis=0, keepdims=True)
        i_row = jnp.sum(jnp.where(ti == si, i_col, 0.0), axis=0, keepdims=True)
        b_col = jnp.sum(jnp.where(ti == si, b_row, 0.0), axis=1, keepdims=True)
        dmat = jnp.where(si <= ti, b_col - b_row + i_row, -jnp.inf)
        inter = b_col + m_old
        mt = jnp.maximum(inter, jnp.max(dmat, axis=1, keepdims=True))
        q = q_all[:, h * dk:(h + 1) * dk]
        k = k_all[:, h * dk:(h + 1) * dk]
        v = v_all[:, h * dv:(h + 1) * dv]
        qk = lax.dot_general(q.astype(BF16), k.astype(BF16), (((1,), (1,)), ((), ())),
                             preferred_element_type=F32) * scale
        s = qk * jnp.exp(dmat - mt)
        w_inter = jnp.exp(inter - mt)
        c_h = c_scr[h]
        num = w_inter * (_mxu(q, c_h) * scale) + _mxu(s, v)
        qn = jnp.sum(q.astype(F32) * n_scr[h:h + 1, :], axis=1, keepdims=True) * scale
        den = w_inter * qn + jnp.sum(s, axis=1, keepdims=True)
        hh = num / jnp.maximum(jnp.abs(den), jnp.exp(-mt))
        hn = hh * lax.rsqrt(jnp.mean(hh * hh, axis=1, keepdims=True) + EPS) * g_ref[:, h * dv:(h + 1) * dv]
        bm = hn * jax.nn.sigmoid(o_all[:, h * dv:(h + 1) * dv])
        if len(bm_ref.shape) == 3:
            bm_ref[0, :, h * dv:(h + 1) * dv] = bm.astype(bm_ref.dtype)
        else:
            bm_ref[:, h * dv:(h + 1) * dv] = bm.astype(bm_ref.dtype)
        bl = b_row[:, L - 1:L]
        wlog = bl - b_col + i_col
        m_new = jnp.maximum(bl + m_old, jnp.max(wlog, axis=0, keepdims=True))
        decay = jnp.exp(bl + m_old - m_new)
        kw = k.astype(F32) * jnp.exp(wlog - m_new)
        upd = lax.dot_general(kw.astype(BF16), v.astype(BF16), (((0,), (0,)), ((), ())),
                              preferred_element_type=F32)
        c_scr[h] = decay * c_h + upd
        n_scr[h:h + 1, :] = decay * n_scr[h:h + 1, :] + jnp.sum(kw, axis=0, keepdims=True)
        m_scr[:, h:h + 1] = m_new

    @pl.when(c == nc - 1)
    def _():
        c_out[0] = c_scr[...]
        n_out[0] = n_scr[...]
        m_out[0] = m_scr[...]


def _mlstm(q, k, v, o, pre_if, g_hnorm, n_batch, t_len, state=None):
    has_init = state is not None
    dk = q.shape[-1] // M_HEADS
    dv = v.shape[-1] // M_HEADS
    L = CHUNK if t_len % CHUNK == 0 else t_len
    nc = t_len // L

    if has_init:
        def tok(width):
            return pl.BlockSpec((1, L, width), lambda b, c: (b, 0, 0))
        bm_shape = (n_batch, t_len, M_HEADS * dv)
    else:
        def tok(width):
            return pl.BlockSpec((L, width), lambda b, c: (b * nc + c, 0))
        bm_shape = (n_batch * t_len, M_HEADS * dv)

    in_specs = [tok(M_HEADS * dk), tok(M_HEADS * dk), tok(M_HEADS * dv), tok(M_HEADS * dv), tok(LANES),
                pl.BlockSpec((1, M_HEADS * dv), lambda b, c: (0, 0))]
    args = [q, k, v, o, pre_if, g_hnorm.reshape(1, -1)]
    c_spec = pl.BlockSpec((1, M_HEADS, dk, dv), lambda b, c: (b, 0, 0, 0))
    n_spec = pl.BlockSpec((1, M_HEADS, dk), lambda b, c: (b, 0, 0))
    m_spec = pl.BlockSpec((1, 1, M_HEADS), lambda b, c: (b, 0, 0))
    if has_init:
        c0, n0, m0 = state
        in_specs += [c_spec, n_spec, m_spec]
        args += [c0, n0, m0.reshape(n_batch, 1, M_HEADS)]
    bm, c_new, n_new, m_new = pl.pallas_call(
        functools.partial(_mlstm_body, L=L, dk=dk, dv=dv, has_init=has_init, nc=nc),
        grid=(n_batch, nc),
        in_specs=in_specs,
        out_specs=[tok(M_HEADS * dv), c_spec, n_spec, m_spec],
        out_shape=[jax.ShapeDtypeStruct(bm_shape, BF16),
                   jax.ShapeDtypeStruct((n_batch, M_HEADS, dk, dv), F32),
                   jax.ShapeDtypeStruct((n_batch, M_HEADS, dk), F32),
                   jax.ShapeDtypeStruct((n_batch, 1, M_HEADS), F32)],
        scratch_shapes=[pltpu.VMEM((M_HEADS, dk, dv), F32), pltpu.VMEM((M_HEADS, dk), F32),
                        pltpu.VMEM((1, M_HEADS), F32)],
        compiler_params=_params(("arbitrary", "arbitrary"), 48),
        name="mlstm_state" if has_init else "mlstm_prompt",
    )(*args)
    return bm, c_new, n_new, m_new.reshape(n_batch, M_HEADS)


def _merge_body(ap_ref, as_ref, bp_ref, bs_ref, wa_ref, wb_ref, ga_ref, gb_ref, o_ref, wab_ref, wbb_ref, *, nbp):
    @pl.when(pl.program_id(1) == 0)
    def _():
        wab_ref[...] = wa_ref[...].astype(BF16)
        wbb_ref[...] = wb_ref[...].astype(BF16)

    is_p = pl.program_id(1) < nbp
    a = jnp.where(is_p, ap_ref[...], as_ref[...])
    bm = jnp.where(is_p, bp_ref[...], bs_ref[...])
    pa = jnp.dot(a, wab_ref[...], preferred_element_type=F32)
    pb = jnp.dot(bm, wbb_ref[...], preferred_element_type=F32)
    merged = jax.nn.sigmoid(ga_ref[...]) * pa + jax.nn.sigmoid(gb_ref[...]) * pb
    o_ref[...] = merged.astype(o_ref.dtype)


def _merge(a_p, a_s, bm_p, bm_s, w_a, w_b, zg, tm):
    ka, kb = a_p.shape[1], bm_p.shape[1]
    nbp = a_p.shape[0] // tm
    m = a_p.shape[0] + a_s.shape[0]
    d = w_a.shape[1]
    tn = min(COL_TILE // 2, d)
    nj = d // tn
    return pl.pallas_call(
        functools.partial(_merge_body, nbp=nbp),
        grid=(nj, m // tm),
        in_specs=[pl.BlockSpec((tm, ka), lambda j, i: (jnp.minimum(i, nbp - 1), 0)),
                  pl.BlockSpec((tm, ka), lambda j, i: (jnp.maximum(i - nbp, 0), 0)),
                  pl.BlockSpec((tm, kb), lambda j, i: (jnp.minimum(i, nbp - 1), 0)),
                  pl.BlockSpec((tm, kb), lambda j, i: (jnp.maximum(i - nbp, 0), 0)),
                  pl.BlockSpec((ka, tn), lambda j, i: (0, j)),
                  pl.BlockSpec((kb, tn), lambda j, i: (0, j)),
                  pl.BlockSpec((tm, tn), lambda j, i: (i, j)),
                  pl.BlockSpec((tm, tn), lambda j, i: (i, nj + j))],
        out_specs=pl.BlockSpec((tm, tn), lambda j, i: (i, j)),
        out_shape=jax.ShapeDtypeStruct((m, d), BF16),
        scratch_shapes=[pltpu.VMEM((ka, tn), BF16), pltpu.VMEM((kb, tn), BF16)],
        compiler_params=_params(("arbitrary", "arbitrary"), 56),
        name="merge",
    )(a_p, a_s, bm_p, bm_s, w_a, w_b, zg, zg)


def _outproj_body(a_ref, w_ref, x_ref, gp_ref, gs_ref, o_ref, wb_ref, *, nbp):
    @pl.when(pl.program_id(1) == 0)
    def _():
        wb_ref[...] = w_ref[...].astype(BF16)

    gt = jnp.where(pl.program_id(1) < nbp, gp_ref[0], gs_ref[...])
    o_ref[...] = x_ref[...] + gt * jnp.dot(a_ref[...], wb_ref[...], preferred_element_type=F32)


def _outproj(a, w, x, gt_p, gt_s, tm, nbp, t_prompt):
    m, k = a.shape
    d = w.shape[1]
    tn = min(COL_TILE, d)
    tpb = t_prompt // tm
    return pl.pallas_call(
        functools.partial(_outproj_body, nbp=nbp),
        grid=(d // tn, m // tm),
        in_specs=[pl.BlockSpec((tm, k), lambda j, i: (i, 0)),
                  pl.BlockSpec((k, tn), lambda j, i: (0, j)),
                  pl.BlockSpec((tm, tn), lambda j, i: (i, j)),
                  pl.BlockSpec((1, 1, tn), lambda j, i: (jnp.minimum(i, nbp - 1) // tpb, 0, j)),
                  pl.BlockSpec((tm, tn), lambda j, i: (jnp.maximum(i - nbp, 0), j))],
        out_specs=pl.BlockSpec((tm, tn), lambda j, i: (i, j)),
        out_shape=jax.ShapeDtypeStruct((m, d), F32),
        scratch_shapes=[pltpu.VMEM((k, tn), BF16)],
        compiler_params=_params(("arbitrary", "arbitrary"), 48),
        name="outproj",
    )(a, w, x, gt_p, gt_s)


def _row_gather(idx_ref, src_hbm, buf_ref, sem, slot):
    n = buf_ref.shape[1]

    def copy(ref, r, sl):
        return pltpu.make_async_copy(src_hbm.at[pl.ds(ref[0, 0, r], 1)], buf_ref.at[sl, pl.ds(r, 1)], sem.at[sl])

    def start_all(ref, sl):
        def body(r, carry):
            copy(ref, r, sl).start()
            return carry
        lax.fori_loop(0, n, body, 0, unroll=8)

    def wait_all():
        def body(r, carry):
            copy(idx_ref, r, slot).wait()
            return carry
        lax.fori_loop(0, n, body, 0, unroll=8)

    return start_all, wait_all


def _gather_body(nut_ref, tok_ref, tok_next_ref, h_hbm, o_ref, buf_ref, sem):
    i = pl.program_id(0)
    n_used = nut_ref[0]
    slot = i % 2
    start_all, wait_all = _row_gather(tok_ref, h_hbm, buf_ref, sem, slot)

    @pl.when(jnp.logical_and(i == 0, n_used > 0))
    def _():
        start_all(tok_ref, 0)

    @pl.when(i + 1 < n_used)
    def _():
        start_all(tok_next_ref, 1 - slot)

    @pl.when(i < n_used)
    def _():
        wait_all()
        o_ref[...] = buf_ref[slot].astype(o_ref.dtype)

    @pl.when(i >= n_used)
    def _():
        o_ref[...] = jnp.zeros(o_ref.shape, o_ref.dtype)


def _gather_rows(h, tok_slot, n_used, rows):
    p = tok_slot.shape[0]
    d = h.shape[1]
    steps = p // rows
    tok3 = tok_slot.reshape(steps, 1, rows)
    grid_spec = pltpu.PrefetchScalarGridSpec(
        num_scalar_prefetch=1,
        grid=(steps,),
        in_specs=[pl.BlockSpec((1, 1, rows), lambda i, nu: (i, 0, 0), memory_space=pltpu.SMEM),
                  pl.BlockSpec((1, 1, rows), lambda i, nu: (jnp.minimum(i + 1, steps - 1), 0, 0),
                               memory_space=pltpu.SMEM),
                  pl.BlockSpec(memory_space=pl.ANY)],
        out_specs=pl.BlockSpec((rows, d), lambda i, nu: (i, 0)),
        scratch_shapes=[pltpu.VMEM((2, rows, d), F32), pltpu.SemaphoreType.DMA((2,))],
    )
    return pl.pallas_call(
        _gather_body,
        grid_spec=grid_spec,
        out_shape=jax.ShapeDtypeStruct((p, d), BF16),
        compiler_params=_params(("arbitrary",), 32),
        name="moe_gather",
    )(n_used, tok3, tok3, h)


def _segment_io(x_hbm, o_hbm, xbuf, obuf, zbuf, sem_x, sem_o, sem_z, pend, st0_ref, snt_ref, nut_ref,
                tm, tn, n_tiles, n_col_tiles):
    s, j = pl.program_id(0), pl.program_id(1)
    nt, t0 = snt_ref[s], st0_ref[s]

    def x_copy(r):
        return pltpu.make_async_copy(x_hbm.at[pl.ds((t0 + r) * tm, tm)], xbuf.at[pl.ds(r * tm, tm)], sem_x)

    def o_copy(slot, t, jj):
        return pltpu.make_async_copy(obuf.at[slot], o_hbm.at[pl.ds(t * tm, tm), pl.ds(jj * tn, tn)], sem_o.at[slot])

    def z_copy(t, jj):
        return pltpu.make_async_copy(zbuf, o_hbm.at[pl.ds(t * tm, tm), pl.ds(jj * tn, tn)], sem_z)

    @pl.when(jnp.logical_and(s == 0, j == 0))
    def _():
        pend[0] = 0
        pend[1] = 0

    def load_rows():
        @pl.when(jnp.logical_and(j == 0, nt > 0))
        def _():
            def start(r, carry):
                x_copy(r).start()
                return carry

            def wait(r, carry):
                x_copy(r).wait()
                return carry

            lax.fori_loop(0, nt, start, 0)
            lax.fori_loop(0, nt, wait, 0)

    def emit(r, value):
        slot = r % 2

        @pl.when(pend[slot] == 1)
        def _():
            o_copy(slot, 0, 0).wait()

        obuf[slot] = value.astype(obuf.dtype)
        o_copy(slot, t0 + r, j).start()
        pend[slot] = 1

    def finish():
        @pl.when(jnp.logical_and(s == pl.num_programs(0) - 1, j == pl.num_programs(1) - 1))
        def _():
            for slot in range(2):
                @pl.when(pend[slot] == 1)
                def _():
                    o_copy(slot, 0, 0).wait()
                    pend[slot] = 0

            zbuf[...] = jnp.zeros(zbuf.shape, zbuf.dtype)

            def start(t, carry):
                for jj in range(n_col_tiles):
                    z_copy(t, jj).start()
                return carry

            def wait(t, carry):
                for jj in range(n_col_tiles):
                    z_copy(t, jj).wait()
                return carry

            lax.fori_loop(nut_ref[0], n_tiles, start, 0)
            lax.fori_loop(nut_ref[0], n_tiles, wait, 0)

    return nt, load_rows, emit, finish


def _expert_up_body(se_ref, st0_ref, snt_ref, nut_ref, x_hbm, wg_ref, wl_ref, bg_ref, bl_ref, o_hbm,
                    xbuf, wgb_ref, wlb_ref, obuf, zbuf, sem_x, sem_o, sem_z, pend, *, tm, tn, n_tiles, n_col_tiles):
    nt, load_rows, emit, finish = _segment_io(x_hbm, o_hbm, xbuf, obuf, zbuf, sem_x, sem_o, sem_z, pend,
                                              st0_ref, snt_ref, nut_ref, tm, tn, n_tiles, n_col_tiles)
    load_rows()

    @pl.when(nt > 0)
    def _():
        wgb_ref[...] = wg_ref[0].astype(BF16)
        wlb_ref[...] = wl_ref[0].astype(BF16)

        def row_tile(r, carry):
            x = xbuf[pl.ds(pl.multiple_of(r * tm, tm), tm), :]
            up_g = jnp.dot(x, wgb_ref[...], preferred_element_type=F32) + bg_ref[0]
            up_l = jnp.dot(x, wlb_ref[...], preferred_element_type=F32) + bl_ref[0]
            glu = jnp.minimum(up_g, SWIGLU_LIMIT)
            lin = jnp.clip(up_l, -SWIGLU_LIMIT, SWIGLU_LIMIT)
            emit(r, glu * jax.nn.sigmoid(SWIGLU_ALPHA * glu) * (lin + 1.0))
            return carry

        lax.fori_loop(0, nt, row_tile, 0)

    finish()


def _expert_down_body(se_ref, st0_ref, snt_ref, nut_ref, a_hbm, w_ref, b_ref, o_hbm,
                      abuf, wb_ref, obuf, zbuf, sem_x, sem_o, sem_z, pend, *, tm, tn, n_tiles, n_col_tiles):
    nt, load_rows, emit, finish = _segment_io(a_hbm, o_hbm, abuf, obuf, zbuf, sem_x, sem_o, sem_z, pend,
                                              st0_ref, snt_ref, nut_ref, tm, tn, n_tiles, n_col_tiles)
    load_rows()

    @pl.when(nt > 0)
    def _():
        wb_ref[...] = w_ref[0].astype(BF16)

        def row_tile(r, carry):
            a = abuf[pl.ds(pl.multiple_of(r * tm, tm), tm), :]
            emit(r, jnp.dot(a, wb_ref[...], preferred_element_type=F32) + b_ref[0])
            return carry

        lax.fori_loop(0, nt, row_tile, 0)

    finish()


def _expert_matmul(body, rows, weights, biases, col_offsets, n_cols, tn, out_dtype, route, tm, name):
    seg_e, seg_t0, seg_nt, n_used = route
    p, k = rows.shape
    n_seg = seg_e.shape[0]
    nj = n_cols // tn
    n_tiles = p // tm

    def w_index(off):
        return lambda s, j, se, st0, snt, nu: (se[s], 0, off + jnp.where(snt[s] > 0, j, nj - 1))

    in_specs = [pl.BlockSpec(memory_space=pl.ANY)]
    in_specs += [pl.BlockSpec((1, k, tn), w_index(off)) for off in col_offsets]
    in_specs += [pl.BlockSpec((1, 1, tn), w_index(off)) for off in col_offsets]
    n_w = len(weights)
    scratch = [pltpu.VMEM((SEGMENT_TILES * tm, k), BF16)]
    scratch += [pltpu.VMEM((k, tn), BF16) for _ in range(n_w)]
    scratch += [pltpu.VMEM((2, tm, tn), out_dtype), pltpu.VMEM((tm, tn), out_dtype),
                pltpu.SemaphoreType.DMA, pltpu.SemaphoreType.DMA((2,)), pltpu.SemaphoreType.DMA,
                pltpu.SMEM((2,), jnp.int32)]
    grid_spec = pltpu.PrefetchScalarGridSpec(
        num_scalar_prefetch=4,
        grid=(n_seg, nj),
        in_specs=in_specs,
        out_specs=pl.BlockSpec(memory_space=pl.ANY),
        scratch_shapes=scratch,
    )
    return pl.pallas_call(
        functools.partial(body, tm=tm, tn=tn, n_tiles=n_tiles, n_col_tiles=nj),
        grid_spec=grid_spec,
        out_shape=jax.ShapeDtypeStruct((p, n_cols), out_dtype),
        compiler_params=_params(("arbitrary", "arbitrary"), 48),
        name=name,
    )(seg_e, seg_t0, seg_nt, n_used, rows, *weights, *biases)


def _expert_up(xs, w_up, b_up, route, tm):
    ne, _, two_de = w_up.shape
    de = two_de // 2
    tn = min(EXPERT_COL_TILE, de)
    b3 = b_up.reshape(ne, 1, two_de)
    return _expert_matmul(_expert_up_body, xs, (w_up, w_up), (b3, b3), (0, de // tn), de, tn, BF16, route, tm,
                          "expert_up")


def _expert_down(act, w_down, b_down, route, tm):
    ne, _, d = w_down.shape
    tn = min(COL_TILE, d)
    return _expert_matmul(_expert_down_body, act, (w_down,), (b_down.reshape(ne, 1, d),), (0,), d, tn, F32,
                          route, tm, "expert_down")


def _combine_body(pos_ref, pos_next_ref, y_hbm, x_ref, gate_ref, gp_ref, gs_ref, gf_ref, op_ref, os_ref,
                  buf_ref, sem, *, nbp):
    i = pl.program_id(0)
    tmc = x_ref.shape[0]
    slot = i % 2
    start_all, wait_all = _row_gather(pos_ref, y_hbm, buf_ref, sem, slot)

    @pl.when(i == 0)
    def _():
        start_all(pos_ref, 0)

    @pl.when(i + 1 < pl.num_programs(0))
    def _():
        start_all(pos_next_ref, 1 - slot)

    wait_all()
    gates = gate_ref[...]
    rows = buf_ref.at[slot]
    y = gates[:, 0:1] * rows[0:tmc, :]
    for k in range(1, TOP_K):
        y = y + gates[:, k:k + 1] * rows[k * tmc:(k + 1) * tmc, :]
    gt = jnp.where(i < nbp, gp_ref[0], gs_ref[...])
    x = x_ref[...] + gt * y
    out = x * lax.rsqrt(jnp.mean(x * x, axis=-1, keepdims=True) + EPS) * gf_ref[...]

    @pl.when(i < nbp)
    def _():
        op_ref[...] = out

    @pl.when(i >= nbp)
    def _():
        os_ref[...] = out


def _combine_final(y_sorted, pos, x, gates, gt_p, gt_s, g_final, tmc, nbp, t_prompt):
    m, d = x.shape
    steps = m // tmc
    tpb = t_prompt // tmc
    pos_blk = pos.reshape(steps, tmc, TOP_K).transpose(0, 2, 1).reshape(steps, 1, TOP_K * tmc)
    idx = pl.BlockSpec((1, 1, TOP_K * tmc), lambda i: (i, 0, 0), memory_space=pltpu.SMEM)
    idx_next = pl.BlockSpec((1, 1, TOP_K * tmc), lambda i: (jnp.minimum(i + 1, steps - 1), 0, 0),
                            memory_space=pltpu.SMEM)
    return pl.pallas_call(
        functools.partial(_combine_body, nbp=nbp),
        grid=(steps,),
        in_specs=[idx, idx_next,
                  pl.BlockSpec(memory_space=pl.ANY),
                  pl.BlockSpec((tmc, d), lambda i: (i, 0)),
                  pl.BlockSpec((tmc, LANES), lambda i: (i, 0)),
                  pl.BlockSpec((1, 1, d), lambda i: (jnp.minimum(i, nbp - 1) // tpb, 0, 0)),
                  pl.BlockSpec((tmc, d), lambda i: (jnp.maximum(i - nbp, 0), 0)),
                  pl.BlockSpec((1, d), lambda i: (0, 0))],
        out_specs=[pl.BlockSpec((tmc, d), lambda i: (jnp.minimum(i, nbp - 1), 0)),
                   pl.BlockSpec((tmc, d), lambda i: (jnp.maximum(i - nbp, 0), 0))],
        out_shape=[jax.ShapeDtypeStruct((nbp * tmc, d), F32), jax.ShapeDtypeStruct((m - nbp * tmc, d), F32)],
        scratch_shapes=[pltpu.VMEM((2, TOP_K * tmc, d), F32), pltpu.SemaphoreType.DMA((2,))],
        compiler_params=_params(("arbitrary",), 32),
        name="moe_combine_norm",
    )(pos_blk, pos_blk, y_sorted, x, gates, gt_p, gt_s, g_final.reshape(1, d))


def _route(top_idx, n_experts, tm):
    t = top_idx.shape[0]
    a = t * TOP_K
    experts = jnp.arange(n_experts, dtype=jnp.int32)
    flat_e = top_idx.reshape(a)
    order = jnp.argsort(flat_e, stable=True).astype(jnp.int32)
    rank = jnp.argsort(order).astype(jnp.int32)
    counts = jnp.sum((flat_e[:, None] == experts[None, :]).astype(jnp.int32), axis=0)
    tiles = (counts + tm - 1) // tm
    tile_end = jnp.cumsum(tiles)
    tile_start = tile_end - tiles
    start = jnp.cumsum(counts) - counts
    pos = tile_start[flat_e] * tm + rank - start[flat_e]
    n_tiles = -(-a // tm) + n_experts
    slot = jnp.arange(n_tiles * tm, dtype=jnp.int32)
    slot_e = jnp.minimum(jnp.sum((tile_end[None, :] * tm <= slot[:, None]).astype(jnp.int32), axis=1), n_experts - 1)
    offset = slot - tile_start[slot_e] * tm
    owned = jnp.logical_and(offset < counts[slot_e], slot < tile_end[-1] * tm)
    tok_slot = jnp.where(owned, order[jnp.clip(start[slot_e] + offset, 0, a - 1)] // TOP_K, 0)
    segs = (tiles + SEGMENT_TILES - 1) // SEGMENT_TILES
    seg_end = jnp.cumsum(segs)
    seg_start = seg_end - segs
    n_seg = n_experts + -(-n_tiles // SEGMENT_TILES)
    s = jnp.arange(n_seg, dtype=jnp.int32)
    s_live = jnp.minimum(s, seg_end[-1] - 1)
    seg_e = jnp.minimum(jnp.sum((seg_end[None, :] <= s_live[:, None]).astype(jnp.int32), axis=1), n_experts - 1)
    k = s_live - seg_start[seg_e]
    seg_nt = jnp.where(s < seg_end[-1], jnp.clip(tiles[seg_e] - k * SEGMENT_TILES, 0, SEGMENT_TILES), 0)
    seg_t0 = tile_start[seg_e] + k * SEGMENT_TILES

    def i32(v):
        return v.astype(jnp.int32)

    return i32(tok_slot), i32(pos), (i32(seg_e), i32(seg_t0), i32(seg_nt), i32(tile_end[-1]).reshape(1))


def _tile_rows(*lengths):
    tm = ROW_TILE
    while any(n % tm for n in lengths):
        tm //= 2
    return tm


def kernel(x_prompt, x_sample, state_pool, state_C, state_n, state_m, c_prompt, c_sample, w_ada, b_ada, g_norm1,
           w_in, b_in, w_pool_grp, pool_scale, w_proj_a, g_hnorm, w_proj_b, w_out, g_norm2, w_router, b_router,
           w_up, b_up, w_down, b_down, g_final):
    bp, tp, d = x_prompt.shape
    bs, ts, _ = x_sample.shape
    assert w_ada.shape[0] == 1, "one decoder layer is supported"
    sdt, pdt = state_C.dtype, state_pool.dtype
    (state_pool, state_C, state_n, state_m, w_ada, b_ada, g_norm1, w_in, b_in, w_pool_grp, pool_scale, w_proj_a,
     g_hnorm, w_proj_b, w_out, g_norm2, w_router, b_router, w_up, b_up, w_down, b_down) = [
        p.reshape(p.shape[1:]) for p in
        (state_pool, state_C, state_n, state_m, w_ada, b_ada, g_norm1, w_in, b_in, w_pool_grp, pool_scale,
         w_proj_a, g_hnorm, w_proj_b, w_out, g_norm2, w_router, b_router, w_up, b_up, w_down, b_down)]
    mp, ms = bp * tp, bs * ts
    tm = _tile_rows(tp, ms)
    tmc = min(ELEMENTWISE_ROWS, tm)
    nbp = mp // tm
    pool_w = w_pool_grp.shape[0] * w_pool_grp.shape[1]
    n_prev = state_pool.shape[1]
    dv = d // M_HEADS
    qk_w = w_in.shape[1] - pool_w - 2 * M_HEADS * dv - 2 * M_HEADS - 2 * d
    qk_w //= 2
    offs = {"u": 0, "q": pool_w, "k": pool_w + qk_w, "v": pool_w + 2 * qk_w}
    offs["o"] = offs["v"] + M_HEADS * dv
    offs["if"] = offs["o"] + M_HEADS * dv
    offs["gab"] = offs["if"] + 2 * M_HEADS

    c_all = jnp.concatenate([c_prompt, c_sample], axis=0)
    n_c = c_all.shape[0]
    c_all = jnp.pad(c_all, ((0, (-n_c) % 8), (0, 0)))
    mod = _ada(c_all, w_ada, b_ada)

    def mod_part(kind):
        part = mod[:, kind * d:(kind + 1) * d]
        return part[:bp].reshape(bp, 1, d), jnp.repeat(part[bp:n_c], ts, axis=0)

    (sh1_p, sh1_s), (sc1_p, sc1_s), (gt1_p, gt1_s) = mod_part(0), mod_part(1), mod_part(2)
    (sh2_p, sh2_s), (sc2_p, sc2_s), (gt2_p, gt2_s) = mod_part(3), mod_part(4), mod_part(5)

    h1, x_all = _norm1(x_prompt.reshape(mp, d), x_sample.reshape(ms, d), sc1_p, sh1_p, sc1_s, sh1_s,
                       g_norm1, tmc, tp)

    w_in_t = w_in.T

    def in_proj(name, n, dtype):
        return _linear_t(h1, w_in_t, b_in[offs[name]:offs[name] + n], offs[name], n, dtype, tm)

    u = in_proj("u", pool_w, F32)
    q = in_proj("q", qk_w, BF16)
    k = in_proj("k", qk_w, F32)
    v = in_proj("v", M_HEADS * dv, BF16)
    o = in_proj("o", M_HEADS * dv, F32)
    pre_if = in_proj("if", LANES, F32)
    zg = in_proj("gab", 2 * d, F32)

    a_p = _pool_prompt(u, w_pool_grp, pool_scale, bp, tp, tm)
    u_s = u[mp:].reshape(bs, ts, pool_w)
    padded_s = jnp.concatenate([state_pool, u_s], axis=1)
    a_s = _pool_sample(padded_s.transpose(1, 0, 2), w_pool_grp, pool_scale, ts)
    a_s = a_s.transpose(1, 0, 2).reshape(ms, pool_w)
    pool_p = u[:mp].reshape(bp, tp, pool_w)[:, tp - n_prev:, :]
    pool_s = padded_s[:, ts:, :]

    bm_p, c_p, n_p, m_p = _mlstm(q, k, v, o, pre_if, g_hnorm, bp, tp)

    def sample_rows(arr):
        return arr[mp:].reshape(bs, ts, arr.shape[1])

    bm_s, c_s, n_s, m_s = _mlstm(sample_rows(q), sample_rows(k), sample_rows(v), sample_rows(o),
                                 sample_rows(pre_if), g_hnorm, bs, ts,
                                 state=(state_C, state_n, state_m))

    merged = _merge(a_p, a_s, bm_p, bm_s.reshape(ms, M_HEADS * dv), w_proj_a, w_proj_b, zg, tm)
    x1 = _outproj(merged, w_out, x_all, gt1_p, gt1_s, tm, nbp, tp)

    h2, top_idx, gates = _norm2_router(x1, sc2_p, sh2_p, sc2_s, sh2_s, g_norm2, w_router, b_router,
                                       tmc, mp // tmc, tp)
    n_experts = w_router.shape[1]
    te = EXPERT_ROW_TILE
    tok_slot, pos, route = _route(top_idx[:, :TOP_K], n_experts, te)
    xs = _gather_rows(h2, tok_slot, route[3], te)
    act = _expert_up(xs, w_up, b_up, route, te)
    y_sorted = _expert_down(act, w_down, b_down, route, te)
    y_p, y_s = _combine_final(y_sorted, pos, x1, gates, gt2_p, gt2_s, g_final, tmc, mp // tmc, tp)

    return (y_p.reshape(bp, tp, d), y_s.reshape(bs, ts, d),
            pool_p.astype(pdt)[None], pool_s.astype(pdt)[None],
            c_p.astype(sdt)[None], c_s.astype(sdt)[None],
            n_p.astype(sdt)[None], n_s.astype(sdt)[None],
            m_p.astype(sdt)[None], m_s.astype(sdt)[None])
```

```python
import functools

import jax
import jax.numpy as jnp
from jax import lax
from jax.experimental import pallas as pl
from jax.experimental.pallas import tpu as pltpu

F32 = jnp.float32
BF16 = jnp.bfloat16

EPS = 1e-6
N_MOD = 6
POOL_WINDOWS = (2, 4, 8, 16)
POOL_HALO = 16
M_HEADS = 8
CHUNK = 64
GATE_CAP = 15.0
PAST_LEN = 16384
TOP_K = 4
SWIGLU_LIMIT = 7.0
SWIGLU_ALPHA = 1.702
LANES = 128
SUBLANES = 8
MIB = 1024 * 1024

ROW_TILE = 512
COL_TILE = 512
EXPERT_ROW_TILE = 128
EXPERT_COL_TILE = 256
SEGMENT_TILES = 16
ITER_TILES = 4
OUT_SLOTS = 4
GATHER_TILES = 2
ELEMENTWISE_ROWS = 64


def _params(semantics, vmem_mib):
    return pltpu.CompilerParams(dimension_semantics=semantics, vmem_limit_bytes=vmem_mib * MIB)


def _mxu(a, b):
    return jnp.dot(a.astype(BF16), b.astype(BF16), preferred_element_type=F32)


def _ada_body(c_ref, w_ref, b_ref, o_ref):
    s = jax.nn.silu(c_ref[...])
    o_ref[...] = _mxu(s, w_ref[...]) + b_ref[...]


def _ada(c, w, b):
    mc, d = c.shape
    n = w.shape[1]
    tn = min(COL_TILE, n)
    return pl.pallas_call(
        _ada_body,
        grid=(n // tn,),
        in_specs=[pl.BlockSpec((mc, d), lambda j: (0, 0)),
                  pl.BlockSpec((d, tn), lambda j: (0, j)),
                  pl.BlockSpec((1, tn), lambda j: (0, j))],
        out_specs=pl.BlockSpec((mc, tn), lambda j: (0, j)),
        out_shape=jax.ShapeDtypeStruct((mc, n), F32),
        compiler_params=_params(("arbitrary",), 40),
        name="ada_mod",
    )(c, w, b.reshape(1, n))


def _rms_mod(x, g, sc, sh):
    y = x * lax.rsqrt(jnp.mean(x * x, axis=-1, keepdims=True) + EPS)
    return (y * g) * (1.0 + sc) + sh


def _norm1_body(xp_ref, xs_ref, scp_ref, shp_ref, scs_ref, shs_ref, g_ref, h_ref, x_ref, *, nbp):
    is_p = pl.program_id(0) < nbp
    x = jnp.where(is_p, xp_ref[...], xs_ref[...])
    sc = jnp.where(is_p, scp_ref[0], scs_ref[...])
    sh = jnp.where(is_p, shp_ref[0], shs_ref[...])
    h_ref[...] = _rms_mod(x, g_ref[...], sc, sh).astype(h_ref.dtype)
    x_ref[...] = x


def _row_specs(tm, d, nbp, tiles_per_batch):
    xp = pl.BlockSpec((tm, d), lambda i: (jnp.minimum(i, nbp - 1), 0))
    xs = pl.BlockSpec((tm, d), lambda i: (jnp.maximum(i - nbp, 0), 0))
    vp = pl.BlockSpec((1, 1, d), lambda i: (jnp.minimum(i, nbp - 1) // tiles_per_batch, 0, 0))
    return xp, xs, vp, xs


def _norm1(xp, xs, scp, shp, scs, shs, g, tm, t_prompt):
    mp, d = xp.shape
    ms = xs.shape[0]
    nbp, nbs = mp // tm, ms // tm
    sp_x, ss_x, sp_v, ss_v = _row_specs(tm, d, nbp, t_prompt // tm)
    row = pl.BlockSpec((tm, d), lambda i: (i, 0))
    return pl.pallas_call(
        functools.partial(_norm1_body, nbp=nbp),
        grid=(nbp + nbs,),
        in_specs=[sp_x, ss_x, sp_v, sp_v, ss_v, ss_v, pl.BlockSpec((1, d), lambda i: (0, 0))],
        out_specs=[row, row],
        out_shape=[jax.ShapeDtypeStruct((mp + ms, d), BF16), jax.ShapeDtypeStruct((mp + ms, d), F32)],
        compiler_params=_params(("arbitrary",), 56),
        name="norm1_mod",
    )(xp, xs, scp, shp, scs, shs, g.reshape(1, d))


def _norm2_body(x_ref, scp_ref, shp_ref, scs_ref, shs_ref, g_ref, wr_ref, br_ref,
                h_ref, idx_ref, gate_ref, *, nbp):
    is_p = pl.program_id(0) < nbp
    sc = jnp.where(is_p, scp_ref[0], scs_ref[...])
    sh = jnp.where(is_p, shp_ref[0], shs_ref[...])
    h = _rms_mod(x_ref[...], g_ref[...], sc, sh)
    h_ref[...] = h.reshape(h_ref.shape)
    logits = _mxu(h, wr_ref[...]) + br_ref[...]
    lane = lax.broadcasted_iota(jnp.int32, logits.shape, 1)
    vals, idxs = [], []
    for _ in range(TOP_K):
        v = jnp.max(logits, axis=-1, keepdims=True)
        ix = jnp.min(jnp.where(logits == v, lane, LANES), axis=-1, keepdims=True)
        vals.append(v)
        idxs.append(ix)
        logits = jnp.where(lane == ix, -jnp.inf, logits)
    exps = [jnp.exp(v - vals[0]) for v in vals]
    tot = exps[0] + exps[1] + exps[2] + exps[3]
    idx_out = jnp.zeros(lane.shape, jnp.int32)
    gate_out = jnp.zeros(lane.shape, F32)
    for k in range(TOP_K):
        idx_out = jnp.where(lane == k, idxs[k], idx_out)
        gate_out = jnp.where(lane == k, exps[k] / tot, gate_out)
    idx_ref[...] = idx_out
    gate_ref[...] = gate_out


def _norm2_router(x, scp, shp, scs, shs, g, w_router, b_router, tm, nbp, t_prompt):
    m, d = x.shape
    e = w_router.shape[1]
    wr = jnp.zeros((d, LANES), F32).at[:, :e].set(w_router)
    br = jnp.full((1, LANES), -jnp.inf, F32).at[0, :e].set(b_router)
    _, _, sp_v, ss_v = _row_specs(tm, d, nbp, t_prompt // tm)
    row = pl.BlockSpec((tm, d), lambda i: (i, 0))
    small = pl.BlockSpec((tm, LANES), lambda i: (i, 0))
    return pl.pallas_call(
        functools.partial(_norm2_body, nbp=nbp),
        grid=(m // tm,),
        in_specs=[row, sp_v, sp_v, ss_v, ss_v, pl.BlockSpec((1, d), lambda i: (0, 0)),
                  pl.BlockSpec((d, LANES), lambda i: (0, 0)), pl.BlockSpec((1, LANES), lambda i: (0, 0))],
        out_specs=[pl.BlockSpec((tm, d // LANES, LANES), lambda i: (i, 0, 0)), small, small],
        out_shape=[jax.ShapeDtypeStruct((m, d // LANES, LANES), F32), jax.ShapeDtypeStruct((m, LANES), jnp.int32),
                   jax.ShapeDtypeStruct((m, LANES), F32)],
        compiler_params=_params(("arbitrary",), 56),
        name="norm2_router",
    )(x, scp, shp, scs, shs, g.reshape(1, d), wr, br)


def _linear_t_body(a_ref, wt_ref, b_ref, o_ref, wb_ref):
    @pl.when(pl.program_id(1) == 0)
    def _():
        wb_ref[...] = wt_ref[...].astype(BF16)

    acc = lax.dot_general(a_ref[...], wb_ref[...], (((1,), (1,)), ((), ())), preferred_element_type=F32)
    o_ref[...] = (acc + b_ref[...]).astype(o_ref.dtype)


def _linear_t(a, wt, b, row0, n, out_dtype, tm):
    m, k = a.shape
    tn = min(COL_TILE, n)
    return pl.pallas_call(
        _linear_t_body,
        grid=(n // tn, m // tm),
        in_specs=[pl.BlockSpec((tm, k), lambda j, i: (i, 0)),
                  pl.BlockSpec((pl.Element(tn), pl.Element(k)),
                               lambda j, i: (pl.multiple_of(row0 + j * tn, SUBLANES), 0)),
                  pl.BlockSpec((1, tn), lambda j, i: (0, j))],
        out_specs=pl.BlockSpec((tm, tn), lambda j, i: (i, j)),
        out_shape=jax.ShapeDtypeStruct((m, n), out_dtype),
        scratch_shapes=[pltpu.VMEM((tn, k), BF16)],
        compiler_params=_params(("arbitrary", "arbitrary"), 48),
        name="linear_t",
    )(a, wt, b.reshape(1, -1))


def _pool_prompt_body(u_ref, w_ref, sc_ref, a_ref, buf_ref, *, tm, gw):
    t = pl.program_id(1)

    @pl.when(t == 0)
    def _():
        buf_ref[0:POOL_HALO, :] = jnp.zeros((POOL_HALO, buf_ref.shape[1]), F32)

    @pl.when(t > 0)
    def _():
        buf_ref[0:POOL_HALO, :] = buf_ref[tm:tm + POOL_HALO, :]

    buf_ref[POOL_HALO:POOL_HALO + tm, :] = u_ref[...]
    pos = t * tm + lax.broadcasted_iota(jnp.int32, (tm, 1), 0)
    for g, w in enumerate(POOL_WINDOWS):
        lo, hi = g * gw, (g + 1) * gw
        cur = buf_ref[POOL_HALO:POOL_HALO + tm, lo:hi]
        s = cur
        for j in range(1, w):
            s = s + buf_ref[POOL_HALO - j:POOL_HALO - j + tm, lo:hi]
        cnt = jnp.minimum(pos + 1, w).astype(F32)
        d = s / cnt - cur
        a = _mxu(d, w_ref[g]) * sc_ref[:, lo:hi]
        a_ref[:, lo:hi] = a.astype(a_ref.dtype)


def _pool_prompt(u, w_grp, scale, n_batch, t_len, tm):
    width = u.shape[1]
    ng, gw, _ = w_grp.shape
    tpb = t_len // tm
    return pl.pallas_call(
        functools.partial(_pool_prompt_body, tm=tm, gw=gw),
        grid=(n_batch, tpb),
        in_specs=[pl.BlockSpec((tm, width), lambda b, t: (b * tpb + t, 0)),
                  pl.BlockSpec((ng, gw, gw), lambda b, t: (0, 0, 0)),
                  pl.BlockSpec((1, width), lambda b, t: (0, 0))],
        out_specs=pl.BlockSpec((tm, width), lambda b, t: (b * tpb + t, 0)),
        out_shape=jax.ShapeDtypeStruct((n_batch * t_len, width), BF16),
        scratch_shapes=[pltpu.VMEM((tm + POOL_HALO, width), F32)],
        compiler_params=_params(("arbitrary", "arbitrary"), 40),
        name="pool_prompt",
    )(u, w_grp, scale.reshape(1, width))


def _pool_sample_body(p_ref, w_ref, sc_ref, a_ref, *, t_len, n_prev):
    g = pl.program_id(0)
    for t in range(t_len):
        cur = p_ref[n_prev + t]
        s = cur
        j = 1
        d = None
        for gi, w in enumerate(POOL_WINDOWS):
            while j < w:
                s = s + p_ref[n_prev + t - j]
                j += 1
            d_g = s / float(min(PAST_LEN + t + 1, w)) - cur
            d = d_g if d is None else jnp.where(g == gi, d_g, d)
        a = _mxu(d, w_ref[0]) * sc_ref[...]
        a_ref[t] = a.astype(a_ref.dtype)


def _pool_sample(padded_tm, w_grp, scale, t_len):
    rows, bs, width = padded_tm.shape
    ng, gw, _ = w_grp.shape
    return pl.pallas_call(
        functools.partial(_pool_sample_body, t_len=t_len, n_prev=rows - t_len),
        grid=(ng,),
        in_specs=[pl.BlockSpec((rows, bs, gw), lambda g: (0, 0, g)),
                  pl.BlockSpec((1, gw, gw), lambda g: (g, 0, 0)),
                  pl.BlockSpec((1, gw), lambda g: (0, g))],
        out_specs=pl.BlockSpec((t_len, bs, gw), lambda g: (0, 0, g)),
        out_shape=jax.ShapeDtypeStruct((t_len, bs, width), BF16),
        compiler_params=_params(("arbitrary",), 40),
        name="pool_sample",
    )(padded_tm, w_grp, scale.reshape(1, width))


def _mlstm_body(*refs, L, dk, dv, has_init, nc):
    if has_init:
        (q_ref, k_ref, v_ref, o_ref, if_ref, g_ref, c0_ref, n0_ref, m0_ref,
         bm_ref, c_out, n_out, m_out, c_scr, n_scr, m_scr) = refs
    else:
        (q_ref, k_ref, v_ref, o_ref, if_ref, g_ref,
         bm_ref, c_out, n_out, m_out, c_scr, n_scr, m_scr) = refs
    c = pl.program_id(1)

    def rows(ref):
        return ref[0] if len(ref.shape) == 3 else ref[...]

    @pl.when(c == 0)
    def _():
        if has_init:
            c_scr[...] = c0_ref[0]
            n_scr[...] = n0_ref[0]
            m_scr[...] = m0_ref[0]
        else:
            c_scr[...] = jnp.zeros(c_scr.shape, F32)
            n_scr[...] = jnp.zeros(n_scr.shape, F32)
            m_scr[...] = jnp.zeros(m_scr.shape, F32)

    q_all, k_all, v_all, o_all = rows(q_ref), rows(k_ref), rows(v_ref), rows(o_ref)
    pre = rows(if_ref)
    capped = GATE_CAP * jnp.tanh(pre / GATE_CAP)
    col = lax.broadcasted_iota(jnp.int32, pre.shape, 1)
    gates = jnp.where(col < M_HEADS, capped, jax.nn.log_sigmoid(capped))

    ti = lax.broadcasted_iota(jnp.int32, (L, L), 0)
    si = lax.broadcasted_iota(jnp.int32, (L, L), 1)
    scale = dk ** -0.5
    for h in range(M_HEADS):
        i_col = gates[:, h:h + 1]
        f_col = gates[:, M_HEADS + h:M_HEADS + h + 1]
        m_old = m_scr[:, h:h + 1]
        b_row = jnp.sum(jnp.where(ti <= si, f_col, 0.0), axis=0, keepdims=True)
        i_row = jnp.sum(jnp.where(ti == si, i_col, 0.0), axis=0, keepdims=True)
        b_col = jnp.sum(jnp.where(ti == si, b_row, 0.0), axis=1, keepdims=True)
        dmat = jnp.where(si <= ti, b_col - b_row + i_row, -jnp.inf)
        inter = b_col + m_old
        mt = jnp.maximum(inter, jnp.max(dmat, axis=1, keepdims=True))
        q = q_all[:, h * dk:(h + 1) * dk]
        k = k_all[:, h * dk:(h + 1) * dk]
        v = v_all[:, h * dv:(h + 1) * dv]
        qk = lax.dot_general(q.astype(BF16), k.astype(BF16), (((1,), (1,)), ((), ())),
                             preferred_element_type=F32) * scale
        s = qk * jnp.exp(dmat - mt)
        w_inter = jnp.exp(inter - mt)
        c_h = c_scr[h]
        num = w_inter * (_mxu(q, c_h) * scale) + _mxu(s, v)
        qn = jnp.sum(q.astype(F32) * n_scr[h:h + 1, :], axis=1, keepdims=True) * scale
        den = w_inter * qn + jnp.sum(s, axis=1, keepdims=True)
        hh = num / jnp.maximum(jnp.abs(den), jnp.exp(-mt))
        hn = hh * lax.rsqrt(jnp.mean(hh * hh, axis=1, keepdims=True) + EPS) * g_ref[:, h * dv:(h + 1) * dv]
        bm = hn * jax.nn.sigmoid(o_all[:, h * dv:(h + 1) * dv])
        if len(bm_ref.shape) == 3:
            bm_ref[0, :, h * dv:(h + 1) * dv] = bm.astype(bm_ref.dtype)
        else:
            bm_ref[:, h * dv:(h + 1) * dv] = bm.astype(bm_ref.dtype)
        bl = b_row[:, L - 1:L]
        wlog = bl - b_col + i_col
        m_new = jnp.maximum(bl + m_old, jnp.max(wlog, axis=0, keepdims=True))
        decay = jnp.exp(bl + m_old - m_new)
        kw = k.astype(F32) * jnp.exp(wlog - m_new)
        upd = lax.dot_general(kw.astype(BF16), v.astype(BF16), (((0,), (0,)), ((), ())),
                              preferred_element_type=F32)
        c_scr[h] = decay * c_h + upd
        n_scr[h:h + 1, :] = decay * n_scr[h:h + 1, :] + jnp.sum(kw, axis=0, keepdims=True)
        m_scr[:, h:h + 1] = m_new

    @pl.when(c == nc - 1)
    def _():
        c_out[0] = c_scr[...]
        n_out[0] = n_scr[...]
        m_out[0] = m_scr[...]


def _mlstm(q, k, v, o, pre_if, g_hnorm, n_batch, t_len, state=None):
    has_init = state is not None
    dk = q.shape[-1] // M_HEADS
    dv = v.shape[-1] // M_HEADS
    L = CHUNK if t_len % CHUNK == 0 else t_len
    nc = t_len // L

    if has_init:
        def tok(width):
            return pl.BlockSpec((1, L, width), lambda b, c: (b, 0, 0))
        bm_shape = (n_batch, t_len, M_HEADS * dv)
    else:
        def tok(width):
            return pl.BlockSpec((L, width), lambda b, c: (b * nc + c, 0))
        bm_shape = (n_batch * t_len, M_HEADS * dv)

    in_specs = [tok(M_HEADS * dk), tok(M_HEADS * dk), tok(M_HEADS * dv), tok(M_HEADS * dv), tok(LANES),
                pl.BlockSpec((1, M_HEADS * dv), lambda b, c: (0, 0))]
    args = [q, k, v, o, pre_if, g_hnorm.reshape(1, -1)]
    c_spec = pl.BlockSpec((1, M_HEADS, dk, dv), lambda b, c: (b, 0, 0, 0))
    n_spec = pl.BlockSpec((1, M_HEADS, dk), lambda b, c: (b, 0, 0))
    m_spec = pl.BlockSpec((1, 1, M_HEADS), lambda b, c: (b, 0, 0))
    if has_init:
        c0, n0, m0 = state
        in_specs += [c_spec, n_spec, m_spec]
        args += [c0, n0, m0.reshape(n_batch, 1, M_HEADS)]
    bm, c_new, n_new, m_new = pl.pallas_call(
        functools.partial(_mlstm_body, L=L, dk=dk, dv=dv, has_init=has_init, nc=nc),
        grid=(n_batch, nc),
        in_specs=in_specs,
        out_specs=[tok(M_HEADS * dv), c_spec, n_spec, m_spec],
        out_shape=[jax.ShapeDtypeStruct(bm_shape, BF16),
                   jax.ShapeDtypeStruct((n_batch, M_HEADS, dk, dv), F32),
                   jax.ShapeDtypeStruct((n_batch, M_HEADS, dk), F32),
                   jax.ShapeDtypeStruct((n_batch, 1, M_HEADS), F32)],
        scratch_shapes=[pltpu.VMEM((M_HEADS, dk, dv), F32), pltpu.VMEM((M_HEADS, dk), F32),
                        pltpu.VMEM((1, M_HEADS), F32)],
        compiler_params=_params(("arbitrary", "arbitrary"), 48),
        name="mlstm_state" if has_init else "mlstm_prompt",
    )(*args)
    return bm, c_new, n_new, m_new.reshape(n_batch, M_HEADS)


def _merge_body(a_ref, bm_ref, wa_ref, wb_ref, ga_ref, gb_ref, o_ref, wab_ref, wbb_ref):
    @pl.when(pl.program_id(1) == 0)
    def _():
        wab_ref[...] = wa_ref[...].astype(BF16)
        wbb_ref[...] = wb_ref[...].astype(BF16)

    pa = jnp.dot(a_ref[...], wab_ref[...], preferred_element_type=F32)
    pb = jnp.dot(bm_ref[...], wbb_ref[...], preferred_element_type=F32)
    merged = jax.nn.sigmoid(ga_ref[...]) * pa + jax.nn.sigmoid(gb_ref[...]) * pb
    o_ref[...] = merged.astype(o_ref.dtype)


def _merge(a, bm, w_a, w_b, zg, tm):
    m, ka = a.shape
    kb = bm.shape[1]
    d = w_a.shape[1]
    tn = min(COL_TILE, d)
    nj = d // tn
    return pl.pallas_call(
        _merge_body,
        grid=(nj, m // tm),
        in_specs=[pl.BlockSpec((tm, ka), lambda j, i: (i, 0)),
                  pl.BlockSpec((tm, kb), lambda j, i: (i, 0)),
                  pl.BlockSpec((ka, tn), lambda j, i: (0, j)),
                  pl.BlockSpec((kb, tn), lambda j, i: (0, j)),
                  pl.BlockSpec((tm, tn), lambda j, i: (i, j)),
                  pl.BlockSpec((tm, tn), lambda j, i: (i, nj + j))],
        out_specs=pl.BlockSpec((tm, tn), lambda j, i: (i, j)),
        out_shape=jax.ShapeDtypeStruct((m, d), BF16),
        scratch_shapes=[pltpu.VMEM((ka, tn), BF16), pltpu.VMEM((kb, tn), BF16)],
        compiler_params=_params(("arbitrary", "arbitrary"), 56),
        name="merge",
    )(a, bm, w_a, w_b, zg, zg)


def _outproj_body(a_ref, w_ref, x_ref, gp_ref, gs_ref, o_ref, wb_ref, *, nbp):
    @pl.when(pl.program_id(1) == 0)
    def _():
        wb_ref[...] = w_ref[...].astype(BF16)

    gt = jnp.where(pl.program_id(1) < nbp, gp_ref[0], gs_ref[...])
    o_ref[...] = x_ref[...] + gt * jnp.dot(a_ref[...], wb_ref[...], preferred_element_type=F32)


def _outproj(a, w, x, gt_p, gt_s, tm, nbp, t_prompt):
    m, k = a.shape
    d = w.shape[1]
    tn = min(COL_TILE, d)
    tpb = t_prompt // tm
    return pl.pallas_call(
        functools.partial(_outproj_body, nbp=nbp),
        grid=(d // tn, m // tm),
        in_specs=[pl.BlockSpec((tm, k), lambda j, i: (i, 0)),
                  pl.BlockSpec((k, tn), lambda j, i: (0, j)),
                  pl.BlockSpec((tm, tn), lambda j, i: (i, j)),
                  pl.BlockSpec((1, 1, tn), lambda j, i: (jnp.minimum(i, nbp - 1) // tpb, 0, j)),
                  pl.BlockSpec((tm, tn), lambda j, i: (jnp.maximum(i - nbp, 0), j))],
        out_specs=pl.BlockSpec((tm, tn), lambda j, i: (i, j)),
        out_shape=jax.ShapeDtypeStruct((m, d), F32),
        scratch_shapes=[pltpu.VMEM((k, tn), BF16)],
        compiler_params=_params(("arbitrary", "arbitrary"), 48),
        name="outproj",
    )(a, w, x, gt_p, gt_s)


def _row_gather(idx_ref, src_hbm, buf_ref, sem, slot):
    n = buf_ref.shape[1]

    def copy(ref, r, sl):
        return pltpu.make_async_copy(src_hbm.at[pl.ds(ref[0, 0, r], 1)], buf_ref.at[sl, pl.ds(r, 1)], sem.at[sl])

    def start_all(ref, sl):
        def body(r, carry):
            copy(ref, r, sl).start()
            return carry
        lax.fori_loop(0, n, body, 0, unroll=8)

    def wait_all():
        def body(r, carry):
            copy(idx_ref, r, slot).wait()
            return carry
        lax.fori_loop(0, n, body, 0, unroll=8)

    return start_all, wait_all


def _gather_body(nut_ref, tok_ref, tok_next_ref, h_hbm, o_ref, buf_ref, sem):
    i = pl.program_id(0)
    n_used = nut_ref[0]
    slot = i % 2
    start_all, wait_all = _row_gather(tok_ref, h_hbm, buf_ref, sem, slot)

    @pl.when(jnp.logical_and(i == 0, n_used > 0))
    def _():
        start_all(tok_ref, 0)

    @pl.when(i + 1 < n_used)
    def _():
        start_all(tok_next_ref, 1 - slot)

    @pl.when(i < n_used)
    def _():
        wait_all()
        o_ref[...] = buf_ref[slot].reshape(o_ref.shape).astype(o_ref.dtype)

    @pl.when(i >= n_used)
    def _():
        o_ref[...] = jnp.zeros(o_ref.shape, o_ref.dtype)


def _gather_rows(h, tok_slot, n_used, rows):
    p = tok_slot.shape[0]
    slab = h.shape[1:]
    d = slab[0] * slab[1]
    steps = p // rows
    tok3 = tok_slot.reshape(steps, 1, rows)
    grid_spec = pltpu.PrefetchScalarGridSpec(
        num_scalar_prefetch=1,
        grid=(steps,),
        in_specs=[pl.BlockSpec((1, 1, rows), lambda i, nu: (i, 0, 0), memory_space=pltpu.SMEM),
                  pl.BlockSpec((1, 1, rows), lambda i, nu: (jnp.minimum(i + 1, steps - 1), 0, 0),
                               memory_space=pltpu.SMEM),
                  pl.BlockSpec(memory_space=pl.ANY)],
        out_specs=pl.BlockSpec((rows, d), lambda i, nu: (i, 0)),
        scratch_shapes=[pltpu.VMEM((2, rows) + slab, F32), pltpu.SemaphoreType.DMA((2,))],
    )
    return pl.pallas_call(
        _gather_body,
        grid_spec=grid_spec,
        out_shape=jax.ShapeDtypeStruct((p, d), BF16),
        compiler_params=_params(("arbitrary",), 32),
        name="moe_gather",
    )(n_used, tok3, tok3, h)


def _segment_io(x_hbm, o_hbm, xbuf, obuf, zbuf, sem_x, sem_o, sem_z, pend, st0_ref, snt_ref, nut_ref,
                tm, tn, n_tiles, n_col_tiles):
    s, j = pl.program_id(0), pl.program_id(1)
    nt, t0 = snt_ref[s], st0_ref[s]
    n_slots = obuf.shape[0]

    def x_copy(r):
        return pltpu.make_async_copy(x_hbm.at[pl.ds((t0 + r) * tm, tm)], xbuf.at[pl.ds(r * tm, tm)], sem_x.at[r])

    def o_copy(slot, t, jj):
        return pltpu.make_async_copy(obuf.at[slot], o_hbm.at[pl.ds(t * tm, tm), pl.ds(jj * tn, tn)], sem_o.at[slot])

    def z_copy(t, jj):
        return pltpu.make_async_copy(zbuf, o_hbm.at[pl.ds(t * tm, tm), pl.ds(jj * tn, tn)], sem_z)

    @pl.when(jnp.logical_and(s == 0, j == 0))
    def _():
        for slot in range(n_slots):
            pend[slot] = 0

    @pl.when(j == 0)
    def _():
        def start(r, carry):
            x_copy(r).start()
            return carry

        lax.fori_loop(0, nt, start, 0)

    def rows(r, n):
        @pl.when(j == 0)
        def _():
            for t in range(n):
                x_copy(r + t).wait()

        return xbuf[pl.ds(pl.multiple_of(r * tm, tm), n * tm), :]

    def emit(r, value):
        slot = r % n_slots

        @pl.when(pend[slot] == 1)
        def _():
            o_copy(slot, 0, 0).wait()

        obuf[slot] = value.astype(obuf.dtype)
        o_copy(slot, t0 + r, j).start()
        pend[slot] = 1

    def finish():
        @pl.when(jnp.logical_and(s == pl.num_programs(0) - 1, j == pl.num_programs(1) - 1))
        def _():
            for slot in range(n_slots):
                @pl.when(pend[slot] == 1)
                def _(slot=slot):
                    o_copy(slot, 0, 0).wait()
                    pend[slot] = 0

            zbuf[...] = jnp.zeros(zbuf.shape, zbuf.dtype)

            def start(t, carry):
                for jj in range(n_col_tiles):
                    z_copy(t, jj).start()
                return carry

            def wait(t, carry):
                for jj in range(n_col_tiles):
                    z_copy(t, jj).wait()
                return carry

            lax.fori_loop(nut_ref[0], n_tiles, start, 0)
            lax.fori_loop(nut_ref[0], n_tiles, wait, 0)

    return nt, rows, emit, finish


def _for_row_tiles(nt, step):
    n_full = nt // ITER_TILES

    def full(i, carry):
        step(i * ITER_TILES, ITER_TILES)
        return carry

    lax.fori_loop(0, n_full, full, 0)
    done = n_full * ITER_TILES
    rem = nt - done
    n = ITER_TILES // 2
    while n >= 1:
        @pl.when((rem // n) % 2 == 1)
        def _(start=done + (rem // (2 * n)) * (2 * n), n=n):
            step(start, n)

        n //= 2


def _expert_up_body(se_ref, st0_ref, snt_ref, nut_ref, x_hbm, wg_ref, wl_ref, bg_ref, bl_ref, o_hbm,
                    xbuf, wcat_ref, obuf, zbuf, sem_x, sem_o, sem_z, pend, *, tm, tn, n_tiles, n_col_tiles):
    nt, rows, emit, finish = _segment_io(x_hbm, o_hbm, xbuf, obuf, zbuf, sem_x, sem_o, sem_z, pend,
                                         st0_ref, snt_ref, nut_ref, tm, tn, n_tiles, n_col_tiles)

    @pl.when(nt > 0)
    def _():
        wcat_ref[:, 0:tn] = wg_ref[0].astype(BF16)
        wcat_ref[:, tn:2 * tn] = wl_ref[0].astype(BF16)

        def step(r, n):
            up = jnp.dot(rows(r, n), wcat_ref[...], preferred_element_type=F32)
            glu = jnp.minimum(up[:, 0:tn] + bg_ref[0], SWIGLU_LIMIT)
            lin = jnp.clip(up[:, tn:2 * tn] + bl_ref[0], -SWIGLU_LIMIT, SWIGLU_LIMIT)
            act = glu * jax.nn.sigmoid(SWIGLU_ALPHA * glu) * (lin + 1.0)
            for t in range(n):
                emit(r + t, act[t * tm:(t + 1) * tm])

        _for_row_tiles(nt, step)

    finish()


def _expert_down_body(se_ref, st0_ref, snt_ref, nut_ref, a_hbm, w_ref, b_ref, o_hbm,
                      abuf, wb_ref, obuf, zbuf, sem_x, sem_o, sem_z, pend, *, tm, tn, n_tiles, n_col_tiles):
    nt, rows, emit, finish = _segment_io(a_hbm, o_hbm, abuf, obuf, zbuf, sem_x, sem_o, sem_z, pend,
                                         st0_ref, snt_ref, nut_ref, tm, tn, n_tiles, n_col_tiles)

    @pl.when(nt > 0)
    def _():
        wb_ref[...] = w_ref[0].astype(BF16)

        def step(r, n):
            y = jnp.dot(rows(r, n), wb_ref[...], preferred_element_type=F32) + b_ref[0]
            for t in range(n):
                emit(r + t, y[t * tm:(t + 1) * tm])

        _for_row_tiles(nt, step)

    finish()


def _expert_matmul(body, rows, weights, biases, col_offsets, n_cols, tn, out_dtype, route, tm, name):
    seg_e, seg_t0, seg_nt, n_used = route
    p, k = rows.shape
    n_seg = seg_e.shape[0]
    nj = n_cols // tn
    n_tiles = p // tm

    def w_index(off):
        return lambda s, j, se, st0, snt, nu: (se[s], 0, off + jnp.where(snt[s] > 0, j, nj - 1))

    in_specs = [pl.BlockSpec(memory_space=pl.ANY)]
    in_specs += [pl.BlockSpec((1, k, tn), w_index(off)) for off in col_offsets]
    in_specs += [pl.BlockSpec((1, 1, tn), w_index(off)) for off in col_offsets]
    scratch = [pltpu.VMEM((SEGMENT_TILES * tm, k), BF16),
               pltpu.VMEM((k, len(weights) * tn), BF16),
               pltpu.VMEM((OUT_SLOTS, tm, tn), out_dtype),
               pltpu.VMEM((tm, tn), out_dtype),
               pltpu.SemaphoreType.DMA((SEGMENT_TILES,)), pltpu.SemaphoreType.DMA((OUT_SLOTS,)),
               pltpu.SemaphoreType.DMA, pltpu.SMEM((OUT_SLOTS,), jnp.int32)]
    grid_spec = pltpu.PrefetchScalarGridSpec(
        num_scalar_prefetch=4,
        grid=(n_seg, nj),
        in_specs=in_specs,
        out_specs=pl.BlockSpec(memory_space=pl.ANY),
        scratch_shapes=scratch,
    )
    return pl.pallas_call(
        functools.partial(body, tm=tm, tn=tn, n_tiles=n_tiles, n_col_tiles=nj),
        grid_spec=grid_spec,
        out_shape=jax.ShapeDtypeStruct((p, n_cols), out_dtype),
        compiler_params=_params(("arbitrary", "arbitrary"), 48),
        name=name,
    )(seg_e, seg_t0, seg_nt, n_used, rows, *weights, *biases)


def _expert_up(xs, w_up, b_up, route, tm):
    ne, _, two_de = w_up.shape
    de = two_de // 2
    tn = min(EXPERT_COL_TILE, de)
    b3 = b_up.reshape(ne, 1, two_de)
    return _expert_matmul(_expert_up_body, xs, (w_up, w_up), (b3, b3), (0, de // tn), de, tn, BF16, route, tm,
                          "expert_up")


def _expert_down(act, w_down, b_down, route, tm):
    ne, _, d = w_down.shape
    tn = min(COL_TILE, d)
    return _expert_matmul(_expert_down_body, act, (w_down,), (b_down.reshape(ne, 1, d),), (0,), d, tn, F32,
                          route, tm, "expert_down")


def _combine_body(pos_ref, pos_next_ref, y_hbm, x_ref, gate_ref, gp_ref, gs_ref, gf_ref, op_ref, os_ref,
                  buf_ref, sem, *, nbp):
    i = pl.program_id(0)
    tmc = x_ref.shape[0]
    slot = i % 2
    start_all, wait_all = _row_gather(pos_ref, y_hbm, buf_ref, sem, slot)

    @pl.when(i == 0)
    def _():
        start_all(pos_ref, 0)

    @pl.when(i + 1 < pl.num_programs(0))
    def _():
        start_all(pos_next_ref, 1 - slot)

    wait_all()
    gates = gate_ref[...]
    rows = buf_ref.at[slot]
    y = gates[:, 0:1] * rows[0:tmc, :]
    for k in range(1, TOP_K):
        y = y + gates[:, k:k + 1] * rows[k * tmc:(k + 1) * tmc, :]
    gt = jnp.where(i < nbp, gp_ref[0], gs_ref[...])
    x = x_ref[...] + gt * y
    out = x * lax.rsqrt(jnp.mean(x * x, axis=-1, keepdims=True) + EPS) * gf_ref[...]

    @pl.when(i < nbp)
    def _():
        op_ref[...] = out

    @pl.when(i >= nbp)
    def _():
        os_ref[...] = out


def _combine_final(y_sorted, pos, x, gates, gt_p, gt_s, g_final, tmc, nbp, t_prompt):
    m, d = x.shape
    steps = m // tmc
    tpb = t_prompt // tmc
    pos_blk = pos.reshape(steps, tmc, TOP_K).transpose(0, 2, 1).reshape(steps, 1, TOP_K * tmc)
    idx = pl.BlockSpec((1, 1, TOP_K * tmc), lambda i: (i, 0, 0), memory_space=pltpu.SMEM)
    idx_next = pl.BlockSpec((1, 1, TOP_K * tmc), lambda i: (jnp.minimum(i + 1, steps - 1), 0, 0),
                            memory_space=pltpu.SMEM)
    return pl.pallas_call(
        functools.partial(_combine_body, nbp=nbp),
        grid=(steps,),
        in_specs=[idx, idx_next,
                  pl.BlockSpec(memory_space=pl.ANY),
                  pl.BlockSpec((tmc, d), lambda i: (i, 0)),
                  pl.BlockSpec((tmc, LANES), lambda i: (i, 0)),
                  pl.BlockSpec((1, 1, d), lambda i: (jnp.minimum(i, nbp - 1) // tpb, 0, 0)),
                  pl.BlockSpec((tmc, d), lambda i: (jnp.maximum(i - nbp, 0), 0)),
                  pl.BlockSpec((1, d), lambda i: (0, 0))],
        out_specs=[pl.BlockSpec((tmc, d), lambda i: (jnp.minimum(i, nbp - 1), 0)),
                   pl.BlockSpec((tmc, d), lambda i: (jnp.maximum(i - nbp, 0), 0))],
        out_shape=[jax.ShapeDtypeStruct((nbp * tmc, d), F32), jax.ShapeDtypeStruct((m - nbp * tmc, d), F32)],
        scratch_shapes=[pltpu.VMEM((2, TOP_K * tmc, d), F32), pltpu.SemaphoreType.DMA((2,))],
        compiler_params=_params(("arbitrary",), 32),
        name="moe_combine_norm",
    )(pos_blk, pos_blk, y_sorted, x, gates, gt_p, gt_s, g_final.reshape(1, d))


def _route(top_idx, n_experts, tm):
    t = top_idx.shape[0]
    a = t * TOP_K
    experts = jnp.arange(n_experts, dtype=jnp.int32)
    flat_e = top_idx.reshape(a)
    order = jnp.argsort(flat_e, stable=True).astype(jnp.int32)
    rank = jnp.argsort(order).astype(jnp.int32)
    counts = jnp.sum((flat_e[:, None] == experts[None, :]).astype(jnp.int32), axis=0)
    tiles = (counts + tm - 1) // tm
    tile_end = jnp.cumsum(tiles)
    tile_start = tile_end - tiles
    start = jnp.cumsum(counts) - counts
    pos = tile_start[flat_e] * tm + rank - start[flat_e]
    n_tiles = -(-(-(-a // tm) + n_experts) // GATHER_TILES) * GATHER_TILES
    slot = jnp.arange(n_tiles * tm, dtype=jnp.int32)
    slot_e = jnp.minimum(jnp.sum((tile_end[None, :] * tm <= slot[:, None]).astype(jnp.int32), axis=1), n_experts - 1)
    offset = slot - tile_start[slot_e] * tm
    owned = jnp.logical_and(offset < counts[slot_e], slot < tile_end[-1] * tm)
    tok_slot = jnp.where(owned, order[jnp.clip(start[slot_e] + offset, 0, a - 1)] // TOP_K, 0)
    segs = (tiles + SEGMENT_TILES - 1) // SEGMENT_TILES
    seg_end = jnp.cumsum(segs)
    seg_start = seg_end - segs
    n_seg = n_experts + -(-n_tiles // SEGMENT_TILES)
    s = jnp.arange(n_seg, dtype=jnp.int32)
    s_live = jnp.minimum(s, seg_end[-1] - 1)
    seg_e = jnp.minimum(jnp.sum((seg_end[None, :] <= s_live[:, None]).astype(jnp.int32), axis=1), n_experts - 1)
    k = s_live - seg_start[seg_e]
    seg_nt = jnp.where(s < seg_end[-1], jnp.clip(tiles[seg_e] - k * SEGMENT_TILES, 0, SEGMENT_TILES), 0)
    seg_t0 = tile_start[seg_e] + k * SEGMENT_TILES

    def i32(v):
        return v.astype(jnp.int32)

    return i32(tok_slot), i32(pos), (i32(seg_e), i32(seg_t0), i32(seg_nt), i32(tile_end[-1]).reshape(1))


def _tile_rows(*lengths):
    tm = ROW_TILE
    while any(n % tm for n in lengths):
        tm //= 2
    return tm


def kernel(x_prompt, x_sample, state_pool, state_C, state_n, state_m, c_prompt, c_sample, w_ada, b_ada, g_norm1,
           w_in, b_in, w_pool_grp, pool_scale, w_proj_a, g_hnorm, w_proj_b, w_out, g_norm2, w_router, b_router,
           w_up, b_up, w_down, b_down, g_final):
    bp, tp, d = x_prompt.shape
    bs, ts, _ = x_sample.shape
    assert w_ada.shape[0] == 1, "one decoder layer is supported"
    sdt, pdt = state_C.dtype, state_pool.dtype
    (state_pool, state_C, state_n, state_m, w_ada, b_ada, g_norm1, w_in, b_in, w_pool_grp, pool_scale, w_proj_a,
     g_hnorm, w_proj_b, w_out, g_norm2, w_router, b_router, w_up, b_up, w_down, b_down) = [
        p.reshape(p.shape[1:]) for p in
        (state_pool, state_C, state_n, state_m, w_ada, b_ada, g_norm1, w_in, b_in, w_pool_grp, pool_scale,
         w_proj_a, g_hnorm, w_proj_b, w_out, g_norm2, w_router, b_router, w_up, b_up, w_down, b_down)]
    mp, ms = bp * tp, bs * ts
    tm = _tile_rows(tp, ms)
    tmc = min(ELEMENTWISE_ROWS, tm)
    nbp = mp // tm
    pool_w = w_pool_grp.shape[0] * w_pool_grp.shape[1]
    n_prev = state_pool.shape[1]
    dv = d // M_HEADS
    qk_w = w_in.shape[1] - pool_w - 2 * M_HEADS * dv - 2 * M_HEADS - 2 * d
    qk_w //= 2
    offs = {"u": 0, "q": pool_w, "k": pool_w + qk_w, "v": pool_w + 2 * qk_w}
    offs["o"] = offs["v"] + M_HEADS * dv
    offs["if"] = offs["o"] + M_HEADS * dv
    offs["gab"] = offs["if"] + 2 * M_HEADS

    c_all = jnp.concatenate([c_prompt, c_sample], axis=0)
    n_c = c_all.shape[0]
    c_all = jnp.pad(c_all, ((0, (-n_c) % 8), (0, 0)))
    mod = _ada(c_all, w_ada, b_ada)

    def mod_part(kind):
        part = mod[:, kind * d:(kind + 1) * d]
        return part[:bp].reshape(bp, 1, d), jnp.repeat(part[bp:n_c], ts, axis=0)

    (sh1_p, sh1_s), (sc1_p, sc1_s), (gt1_p, gt1_s) = mod_part(0), mod_part(1), mod_part(2)
    (sh2_p, sh2_s), (sc2_p, sc2_s), (gt2_p, gt2_s) = mod_part(3), mod_part(4), mod_part(5)

    h1, x_all = _norm1(x_prompt.reshape(mp, d), x_sample.reshape(ms, d), sc1_p, sh1_p, sc1_s, sh1_s,
                       g_norm1, tmc, tp)

    w_in_t = w_in.T

    def in_proj(name, n, dtype):
        return _linear_t(h1, w_in_t, b_in[offs[name]:offs[name] + n], offs[name], n, dtype, tm)

    u = in_proj("u", pool_w, F32)
    q = in_proj("q", qk_w, BF16)
    k = in_proj("k", qk_w, F32)
    v = in_proj("v", M_HEADS * dv, BF16)
    o = in_proj("o", M_HEADS * dv, F32)
    pre_if = in_proj("if", LANES, F32)
    zg = in_proj("gab", 2 * d, F32)

    a_p = _pool_prompt(u, w_pool_grp, pool_scale, bp, tp, tm)
    u_s = u[mp:].reshape(bs, ts, pool_w)
    padded_s = jnp.concatenate([state_pool, u_s], axis=1)
    a_s = _pool_sample(padded_s.transpose(1, 0, 2), w_pool_grp, pool_scale, ts)
    a_s = a_s.transpose(1, 0, 2).reshape(ms, pool_w)
    pool_p = u[:mp].reshape(bp, tp, pool_w)[:, tp - n_prev:, :]
    pool_s = padded_s[:, ts:, :]

    bm_p, c_p, n_p, m_p = _mlstm(q, k, v, o, pre_if, g_hnorm, bp, tp)

    def sample_rows(arr):
        return arr[mp:].reshape(bs, ts, arr.shape[1])

    bm_s, c_s, n_s, m_s = _mlstm(sample_rows(q), sample_rows(k), sample_rows(v), sample_rows(o),
                                 sample_rows(pre_if), g_hnorm, bs, ts,
                                 state=(state_C, state_n, state_m))

    a_all = jnp.concatenate([a_p, a_s], axis=0)
    bm_all = jnp.concatenate([bm_p, bm_s.reshape(ms, M_HEADS * dv)], axis=0)
    merged = _merge(a_all, bm_all, w_proj_a, w_proj_b, zg, tm)
    x1 = _outproj(merged, w_out, x_all, gt1_p, gt1_s, tm, nbp, tp)

    h2, top_idx, gates = _norm2_router(x1, sc2_p, sh2_p, sc2_s, sh2_s, g_norm2, w_router, b_router,
                                       tmc, mp // tmc, tp)
    n_experts = w_router.shape[1]
    te = EXPERT_ROW_TILE
    tok_slot, pos, route = _route(top_idx[:, :TOP_K], n_experts, te)
    gather_steps = (route[3] + GATHER_TILES - 1) // GATHER_TILES
    xs = _gather_rows(h2, tok_slot, gather_steps, GATHER_TILES * te)
    act = _expert_up(xs, w_up, b_up, route, te)
    y_sorted = _expert_down(act, w_down, b_down, route, te)
    y_p, y_s = _combine_final(y_sorted, pos, x1, gates, gt2_p, gt2_s, g_final, tmc, mp // tmc, tp)

    return (y_p.reshape(bp, tp, d), y_s.reshape(bs, ts, d),
            pool_p.astype(pdt)[None], pool_s.astype(pdt)[None],
            c_p.astype(sdt)[None], c_s.astype(sdt)[None],
            n_p.astype(sdt)[None], n_s.astype(sdt)[None],
            m_p.astype(sdt)[None], m_s.astype(sdt)[None])
```

```python
import functools

import jax
import jax.numpy as jnp
from jax import lax
from jax.experimental import pallas as pl
from jax.experimental.pallas import tpu as pltpu

F32 = jnp.float32
BF16 = jnp.bfloat16

EPS = 1e-6
N_MOD = 6
POOL_WINDOWS = (2, 4, 8, 16)
POOL_HALO = 16
M_HEADS = 8
CHUNK = 64
GATE_CAP = 15.0
PAST_LEN = 16384
TOP_K = 4
SWIGLU_LIMIT = 7.0
SWIGLU_ALPHA = 1.702
LANES = 128
SUBLANES = 8
MIB = 1024 * 1024

ROW_TILE = 512
COL_TILE = 512
EXPERT_ROW_TILE = 128
EXPERT_COL_TILE = 256
SEGMENT_TILES = 16
ITER_TILES = 4
OUT_SLOTS = 4
GATHER_TILES = 2
ELEMENTWISE_ROWS = 64


def _params(semantics, vmem_mib):
    return pltpu.CompilerParams(dimension_semantics=semantics, vmem_limit_bytes=vmem_mib * MIB)


def _mxu(a, b):
    return jnp.dot(a.astype(BF16), b.astype(BF16), preferred_element_type=F32)


def _ada_body(c_ref, w_ref, b_ref, o_ref):
    s = jax.nn.silu(c_ref[...])
    o_ref[...] = _mxu(s, w_ref[...]) + b_ref[...]


def _ada(c, w, b):
    mc, d = c.shape
    n = w.shape[1]
    tn = min(COL_TILE, n)
    return pl.pallas_call(
        _ada_body,
        grid=(n // tn,),
        in_specs=[pl.BlockSpec((mc, d), lambda j: (0, 0)),
                  pl.BlockSpec((d, tn), lambda j: (0, j)),
                  pl.BlockSpec((1, tn), lambda j: (0, j))],
        out_specs=pl.BlockSpec((mc, tn), lambda j: (0, j)),
        out_shape=jax.ShapeDtypeStruct((mc, n), F32),
        compiler_params=_params(("arbitrary",), 40),
        name="ada_mod",
    )(c, w, b.reshape(1, n))


def _rms_mod(x, g, sc, sh):
    y = x * lax.rsqrt(jnp.mean(x * x, axis=-1, keepdims=True) + EPS)
    return (y * g) * (1.0 + sc) + sh


def _norm1_body(xp_ref, xs_ref, scp_ref, shp_ref, scs_ref, shs_ref, g_ref, h_ref, x_ref, *, nbp):
    is_p = pl.program_id(0) < nbp
    x = jnp.where(is_p, xp_ref[...], xs_ref[...])
    sc = jnp.where(is_p, scp_ref[0], scs_ref[...])
    sh = jnp.where(is_p, shp_ref[0], shs_ref[...])
    h_ref[...] = _rms_mod(x, g_ref[...], sc, sh).astype(h_ref.dtype)
    x_ref[...] = x


def _row_specs(tm, d, nbp, tiles_per_batch):
    xp = pl.BlockSpec((tm, d), lambda i: (jnp.minimum(i, nbp - 1), 0))
    xs = pl.BlockSpec((tm, d), lambda i: (jnp.maximum(i - nbp, 0), 0))
    vp = pl.BlockSpec((1, 1, d), lambda i: (jnp.minimum(i, nbp - 1) // tiles_per_batch, 0, 0))
    return xp, xs, vp, xs


def _norm1(xp, xs, scp, shp, scs, shs, g, tm, t_prompt):
    mp, d = xp.shape
    ms = xs.shape[0]
    nbp, nbs = mp // tm, ms // tm
    sp_x, ss_x, sp_v, ss_v = _row_specs(tm, d, nbp, t_prompt // tm)
    row = pl.BlockSpec((tm, d), lambda i: (i, 0))
    return pl.pallas_call(
        functools.partial(_norm1_body, nbp=nbp),
        grid=(nbp + nbs,),
        in_specs=[sp_x, ss_x, sp_v, sp_v, ss_v, ss_v, pl.BlockSpec((1, d), lambda i: (0, 0))],
        out_specs=[row, row],
        out_shape=[jax.ShapeDtypeStruct((mp + ms, d), BF16), jax.ShapeDtypeStruct((mp + ms, d), F32)],
        compiler_params=_params(("arbitrary",), 56),
        name="norm1_mod",
    )(xp, xs, scp, shp, scs, shs, g.reshape(1, d))


def _norm2_body(x_ref, scp_ref, shp_ref, scs_ref, shs_ref, g_ref, wr_ref, br_ref,
                h_ref, idx_ref, gate_ref, *, nbp):
    is_p = pl.program_id(0) < nbp
    sc = jnp.where(is_p, scp_ref[0], scs_ref[...])
    sh = jnp.where(is_p, shp_ref[0], shs_ref[...])
    h = _rms_mod(x_ref[...], g_ref[...], sc, sh)
    h_ref[...] = h.reshape(h_ref.shape)
    logits = _mxu(h, wr_ref[...]) + br_ref[...]
    lane = lax.broadcasted_iota(jnp.int32, logits.shape, 1)
    vals, idxs = [], []
    for _ in range(TOP_K):
        v = jnp.max(logits, axis=-1, keepdims=True)
        ix = jnp.min(jnp.where(logits == v, lane, LANES), axis=-1, keepdims=True)
        vals.append(v)
        idxs.append(ix)
        logits = jnp.where(lane == ix, -jnp.inf, logits)
    exps = [jnp.exp(v - vals[0]) for v in vals]
    tot = exps[0] + exps[1] + exps[2] + exps[3]
    idx_out = jnp.zeros(lane.shape, jnp.int32)
    gate_out = jnp.zeros(lane.shape, F32)
    for k in range(TOP_K):
        idx_out = jnp.where(lane == k, idxs[k], idx_out)
        gate_out = jnp.where(lane == k, exps[k] / tot, gate_out)
    idx_ref[...] = idx_out
    gate_ref[...] = gate_out


def _norm2_router(x, scp, shp, scs, shs, g, w_router, b_router, tm, nbp, t_prompt):
    m, d = x.shape
    e = w_router.shape[1]
    wr = jnp.zeros((d, LANES), F32).at[:, :e].set(w_router)
    br = jnp.full((1, LANES), -jnp.inf, F32).at[0, :e].set(b_router)
    _, _, sp_v, ss_v = _row_specs(tm, d, nbp, t_prompt // tm)
    row = pl.BlockSpec((tm, d), lambda i: (i, 0))
    small = pl.BlockSpec((tm, LANES), lambda i: (i, 0))
    return pl.pallas_call(
        functools.partial(_norm2_body, nbp=nbp),
        grid=(m // tm,),
        in_specs=[row, sp_v, sp_v, ss_v, ss_v, pl.BlockSpec((1, d), lambda i: (0, 0)),
                  pl.BlockSpec((d, LANES), lambda i: (0, 0)), pl.BlockSpec((1, LANES), lambda i: (0, 0))],
        out_specs=[pl.BlockSpec((tm, d // LANES, LANES), lambda i: (i, 0, 0)), small, small],
        out_shape=[jax.ShapeDtypeStruct((m, d // LANES, LANES), F32), jax.ShapeDtypeStruct((m, LANES), jnp.int32),
                   jax.ShapeDtypeStruct((m, LANES), F32)],
        compiler_params=_params(("arbitrary",), 56),
        name="norm2_router",
    )(x, scp, shp, scs, shs, g.reshape(1, d), wr, br)


def _wdot(a, w, w_contract=0):
    return lax.dot_general(a, w, (((1,), (w_contract,)), ((), ())), preferred_element_type=F32)


def _linear_t_body(a_ref, wt_ref, b_ref, o_ref):
    o_ref[...] = (_wdot(a_ref[...], wt_ref[...], 1) + b_ref[...]).astype(o_ref.dtype)


def _linear_t(a, wt, b, row0, n, out_dtype, tm):
    m, k = a.shape
    tn = min(COL_TILE, n)
    return pl.pallas_call(
        _linear_t_body,
        grid=(n // tn, m // tm),
        in_specs=[pl.BlockSpec((tm, k), lambda j, i: (i, 0)),
                  pl.BlockSpec((pl.Element(tn), pl.Element(k)),
                               lambda j, i: (pl.multiple_of(row0 + j * tn, SUBLANES), 0)),
                  pl.BlockSpec((1, tn), lambda j, i: (0, j))],
        out_specs=pl.BlockSpec((tm, tn), lambda j, i: (i, j)),
        out_shape=jax.ShapeDtypeStruct((m, n), out_dtype),
        compiler_params=_params(("arbitrary", "arbitrary"), 48),
        name="linear_t",
    )(a, wt, b.reshape(1, -1))


def _pool_prompt_body(u_ref, w_ref, sc_ref, a_ref, buf_ref, *, tm, gw):
    t = pl.program_id(1)

    @pl.when(t == 0)
    def _():
        buf_ref[0:POOL_HALO, :] = jnp.zeros((POOL_HALO, buf_ref.shape[1]), F32)

    @pl.when(t > 0)
    def _():
        buf_ref[0:POOL_HALO, :] = buf_ref[tm:tm + POOL_HALO, :]

    buf_ref[POOL_HALO:POOL_HALO + tm, :] = u_ref[...]
    pos = t * tm + lax.broadcasted_iota(jnp.int32, (tm, 1), 0)
    for g, w in enumerate(POOL_WINDOWS):
        lo, hi = g * gw, (g + 1) * gw
        cur = buf_ref[POOL_HALO:POOL_HALO + tm, lo:hi]
        s = cur
        for j in range(1, w):
            s = s + buf_ref[POOL_HALO - j:POOL_HALO - j + tm, lo:hi]
        cnt = jnp.minimum(pos + 1, w).astype(F32)
        d = s / cnt - cur
        a = _mxu(d, w_ref[g]) * sc_ref[:, lo:hi]
        a_ref[:, lo:hi] = a.astype(a_ref.dtype)


def _pool_prompt(u, w_grp, scale, n_batch, t_len, tm):
    width = u.shape[1]
    ng, gw, _ = w_grp.shape
    tpb = t_len // tm
    return pl.pallas_call(
        functools.partial(_pool_prompt_body, tm=tm, gw=gw),
        grid=(n_batch, tpb),
        in_specs=[pl.BlockSpec((tm, width), lambda b, t: (b * tpb + t, 0)),
                  pl.BlockSpec((ng, gw, gw), lambda b, t: (0, 0, 0)),
                  pl.BlockSpec((1, width), lambda b, t: (0, 0))],
        out_specs=pl.BlockSpec((tm, width), lambda b, t: (b * tpb + t, 0)),
        out_shape=jax.ShapeDtypeStruct((n_batch * t_len, width), BF16),
        scratch_shapes=[pltpu.VMEM((tm + POOL_HALO, width), F32)],
        compiler_params=_params(("arbitrary", "arbitrary"), 40),
        name="pool_prompt",
    )(u, w_grp, scale.reshape(1, width))


def _pool_sample_body(p_ref, w_ref, sc_ref, a_ref, *, t_len, n_prev):
    g = pl.program_id(0)
    for t in range(t_len):
        cur = p_ref[n_prev + t]
        s = cur
        j = 1
        d = None
        for gi, w in enumerate(POOL_WINDOWS):
            while j < w:
                s = s + p_ref[n_prev + t - j]
                j += 1
            d_g = s / float(min(PAST_LEN + t + 1, w)) - cur
            d = d_g if d is None else jnp.where(g == gi, d_g, d)
        a = _mxu(d, w_ref[0]) * sc_ref[...]
        a_ref[t] = a.astype(a_ref.dtype)


def _pool_sample(padded_tm, w_grp, scale, t_len):
    rows, bs, width = padded_tm.shape
    ng, gw, _ = w_grp.shape
    return pl.pallas_call(
        functools.partial(_pool_sample_body, t_len=t_len, n_prev=rows - t_len),
        grid=(ng,),
        in_specs=[pl.BlockSpec((rows, bs, gw), lambda g: (0, 0, g)),
                  pl.BlockSpec((1, gw, gw), lambda g: (g, 0, 0)),
                  pl.BlockSpec((1, gw), lambda g: (0, g))],
        out_specs=pl.BlockSpec((t_len, bs, gw), lambda g: (0, 0, g)),
        out_shape=jax.ShapeDtypeStruct((t_len, bs, width), BF16),
        compiler_params=_params(("arbitrary",), 40),
        name="pool_sample",
    )(padded_tm, w_grp, scale.reshape(1, width))


def _mlstm_body(*refs, L, dk, dv, has_init, nc):
    if has_init:
        (q_ref, k_ref, v_ref, o_ref, if_ref, g_ref, c0_ref, n0_ref, m0_ref,
         bm_ref, c_out, n_out, m_out, c_scr, n_scr, m_scr) = refs
    else:
        (q_ref, k_ref, v_ref, o_ref, if_ref, g_ref,
         bm_ref, c_out, n_out, m_out, c_scr, n_scr, m_scr) = refs
    c = pl.program_id(1)

    def rows(ref):
        return ref[0] if len(ref.shape) == 3 else ref[...]

    @pl.when(c == 0)
    def _():
        if has_init:
            c_scr[...] = c0_ref[0]
            n_scr[...] = n0_ref[0]
            m_scr[...] = m0_ref[0]
        else:
            c_scr[...] = jnp.zeros(c_scr.shape, F32)
            n_scr[...] = jnp.zeros(n_scr.shape, F32)
            m_scr[...] = jnp.zeros(m_scr.shape, F32)

    q_all, k_all, v_all, o_all = rows(q_ref), rows(k_ref), rows(v_ref), rows(o_ref)
    pre = rows(if_ref)
    capped = GATE_CAP * jnp.tanh(pre / GATE_CAP)
    col = lax.broadcasted_iota(jnp.int32, pre.shape, 1)
    gates = jnp.where(col < M_HEADS, capped, jax.nn.log_sigmoid(capped))

    ti = lax.broadcasted_iota(jnp.int32, (L, L), 0)
    si = lax.broadcasted_iota(jnp.int32, (L, L), 1)
    scale = dk ** -0.5
    for h in range(M_HEADS):
        i_col = gates[:, h:h + 1]
        f_col = gates[:, M_HEADS + h:M_HEADS + h + 1]
        m_old = m_scr[:, h:h + 1]
        n_old = n_scr[h:h + 1, :]
        b_row = jnp.sum(jnp.where(ti <= si, f_col, 0.0), axis=0, keepdims=True)
        i_row = jnp.sum(jnp.where(ti == si, i_col, 0.0), axis=0, keepdims=True)
        b_col = jnp.sum(jnp.where(ti == si, b_row, 0.0), axis=1, keepdims=True)
        dmat = jnp.where(si <= ti, b_col - b_row + i_row, -jnp.inf)
        inter = b_col + m_old
        mt = jnp.maximum(inter, jnp.max(dmat, axis=1, keepdims=True))
        q = q_all[:, h * dk:(h + 1) * dk]
        k = k_all[:, h * dk:(h + 1) * dk]
        v = v_all[:, h * dv:(h + 1) * dv]
        qk = lax.dot_general(q.astype(BF16), k.astype(BF16), (((1,), (1,)), ((), ())),
                             preferred_element_type=F32) * scale
        s = qk * jnp.exp(dmat - mt)
        w_inter = jnp.exp(inter - mt)
        c_h = c_scr[h]
        num = w_inter * (_mxu(q, c_h) * scale) + _mxu(s, v)
        qn = jnp.sum(q.astype(F32) * n_old, axis=1, keepdims=True) * scale
        den = w_inter * qn + jnp.sum(s, axis=1, keepdims=True)
        hh = num / jnp.maximum(jnp.abs(den), jnp.exp(-mt))
        hn = hh * lax.rsqrt(jnp.mean(hh * hh, axis=1, keepdims=True) + EPS) * g_ref[:, h * dv:(h + 1) * dv]
        bm = hn * jax.nn.sigmoid(o_all[:, h * dv:(h + 1) * dv])
        if len(bm_ref.shape) == 3:
            bm_ref[0, :, h * dv:(h + 1) * dv] = bm.astype(bm_ref.dtype)
        else:
            bm_ref[:, h * dv:(h + 1) * dv] = bm.astype(bm_ref.dtype)
        bl = b_row[:, L - 1:L]
        wlog = bl - b_col + i_col
        m_new = jnp.maximum(bl + m_old, jnp.max(wlog, axis=0, keepdims=True))
        decay = jnp.exp(bl + m_old - m_new)
        kw = k.astype(F32) * jnp.exp(wlog - m_new)
        upd = lax.dot_general(kw.astype(BF16), v.astype(BF16), (((0,), (0,)), ((), ())),
                              preferred_element_type=F32)
        c_scr[h] = decay * c_h + upd
        n_scr[h:h + 1, :] = decay * n_old + jnp.sum(kw, axis=0, keepdims=True)
        m_scr[:, h:h + 1] = m_new

    @pl.when(c == nc - 1)
    def _():
        c_out[0] = c_scr[...]
        n_out[0] = n_scr[...]
        m_out[0] = m_scr[...]


def _mlstm(q, k, v, o, pre_if, g_hnorm, n_batch, t_len, state=None):
    has_init = state is not None
    dk = q.shape[-1] // M_HEADS
    dv = v.shape[-1] // M_HEADS
    L = CHUNK if t_len % CHUNK == 0 else t_len
    nc = t_len // L

    if has_init:
        def tok(width):
            return pl.BlockSpec((1, L, width), lambda b, c: (b, 0, 0))
        bm_shape = (n_batch, t_len, M_HEADS * dv)
    else:
        def tok(width):
            return pl.BlockSpec((L, width), lambda b, c: (b * nc + c, 0))
        bm_shape = (n_batch * t_len, M_HEADS * dv)

    in_specs = [tok(M_HEADS * dk), tok(M_HEADS * dk), tok(M_HEADS * dv), tok(M_HEADS * dv), tok(LANES),
                pl.BlockSpec((1, M_HEADS * dv), lambda b, c: (0, 0))]
    args = [q, k, v, o, pre_if, g_hnorm.reshape(1, -1)]
    c_spec = pl.BlockSpec((1, M_HEADS, dk, dv), lambda b, c: (b, 0, 0, 0))
    n_spec = pl.BlockSpec((1, M_HEADS, dk), lambda b, c: (b, 0, 0))
    m_spec = pl.BlockSpec((1, 1, M_HEADS), lambda b, c: (b, 0, 0))
    if has_init:
        c0, n0, m0 = state
        in_specs += [c_spec, n_spec, m_spec]
        args += [c0, n0, m0.reshape(n_batch, 1, M_HEADS)]
    bm, c_new, n_new, m_new = pl.pallas_call(
        functools.partial(_mlstm_body, L=L, dk=dk, dv=dv, has_init=has_init, nc=nc),
        grid=(n_batch, nc),
        in_specs=in_specs,
        out_specs=[tok(M_HEADS * dv), c_spec, n_spec, m_spec],
        out_shape=[jax.ShapeDtypeStruct(bm_shape, BF16),
                   jax.ShapeDtypeStruct((n_batch, M_HEADS, dk, dv), F32),
                   jax.ShapeDtypeStruct((n_batch, M_HEADS, dk), F32),
                   jax.ShapeDtypeStruct((n_batch, 1, M_HEADS), F32)],
        scratch_shapes=[pltpu.VMEM((M_HEADS, dk, dv), F32), pltpu.VMEM((M_HEADS, dk), F32),
                        pltpu.VMEM((1, M_HEADS), F32)],
        compiler_params=_params(("arbitrary", "arbitrary"), 48),
        name="mlstm_state" if has_init else "mlstm_prompt",
    )(*args)
    return bm, c_new, n_new, m_new.reshape(n_batch, M_HEADS)


def _merge_body(a_ref, bm_ref, wa_ref, wb_ref, ga_ref, gb_ref, o_ref):
    pa = _wdot(a_ref[...], wa_ref[...])
    pb = _wdot(bm_ref[...], wb_ref[...])
    merged = jax.nn.sigmoid(ga_ref[...]) * pa + jax.nn.sigmoid(gb_ref[...]) * pb
    o_ref[...] = merged.astype(o_ref.dtype)


def _merge(a, bm, w_a, w_b, zg, tm):
    m, ka = a.shape
    kb = bm.shape[1]
    d = w_a.shape[1]
    tn = min(COL_TILE, d)
    nj = d // tn
    return pl.pallas_call(
        _merge_body,
        grid=(nj, m // tm),
        in_specs=[pl.BlockSpec((tm, ka), lambda j, i: (i, 0)),
                  pl.BlockSpec((tm, kb), lambda j, i: (i, 0)),
                  pl.BlockSpec((ka, tn), lambda j, i: (0, j)),
                  pl.BlockSpec((kb, tn), lambda j, i: (0, j)),
                  pl.BlockSpec((tm, tn), lambda j, i: (i, j)),
                  pl.BlockSpec((tm, tn), lambda j, i: (i, nj + j))],
        out_specs=pl.BlockSpec((tm, tn), lambda j, i: (i, j)),
        out_shape=jax.ShapeDtypeStruct((m, d), BF16),
        compiler_params=_params(("arbitrary", "arbitrary"), 56),
        name="merge",
    )(a, bm, w_a, w_b, zg, zg)


def _outproj_body(a_ref, w_ref, x_ref, gp_ref, gs_ref, o_ref, *, nbp):
    gt = jnp.where(pl.program_id(1) < nbp, gp_ref[0], gs_ref[...])
    o_ref[...] = x_ref[...] + gt * _wdot(a_ref[...], w_ref[...])


def _outproj(a, w, x, gt_p, gt_s, tm, nbp, t_prompt):
    m, k = a.shape
    d = w.shape[1]
    tn = min(COL_TILE, d)
    tpb = t_prompt // tm
    return pl.pallas_call(
        functools.partial(_outproj_body, nbp=nbp),
        grid=(d // tn, m // tm),
        in_specs=[pl.BlockSpec((tm, k), lambda j, i: (i, 0)),
                  pl.BlockSpec((k, tn), lambda j, i: (0, j)),
                  pl.BlockSpec((tm, tn), lambda j, i: (i, j)),
                  pl.BlockSpec((1, 1, tn), lambda j, i: (jnp.minimum(i, nbp - 1) // tpb, 0, j)),
                  pl.BlockSpec((tm, tn), lambda j, i: (jnp.maximum(i - nbp, 0), j))],
        out_specs=pl.BlockSpec((tm, tn), lambda j, i: (i, j)),
        out_shape=jax.ShapeDtypeStruct((m, d), F32),
        compiler_params=_params(("arbitrary", "arbitrary"), 48),
        name="outproj",
    )(a, w, x, gt_p, gt_s)


def _row_gather(idx_ref, src_hbm, buf_ref, sem, slot):
    n = buf_ref.shape[1]

    def copy(ref, r, sl):
        return pltpu.make_async_copy(src_hbm.at[pl.ds(ref[0, 0, r], 1)], buf_ref.at[sl, pl.ds(r, 1)], sem.at[sl])

    def start_all(ref, sl):
        def body(r2, carry):
            copy(ref, 2 * r2, sl).start(priority=0)
            copy(ref, 2 * r2 + 1, sl).start(priority=1)
            return carry
        lax.fori_loop(0, n // 2, body, 0, unroll=4)

    def wait_all():
        def body(r, carry):
            copy(idx_ref, r, slot).wait()
            return carry
        lax.fori_loop(0, n, body, 0, unroll=8)

    return start_all, wait_all


def _gather_body(nut_ref, tok_ref, tok_next_ref, h_hbm, o_ref, buf_ref, sem):
    i = pl.program_id(0)
    n_used = nut_ref[0]
    slot = i % 2
    start_all, wait_all = _row_gather(tok_ref, h_hbm, buf_ref, sem, slot)

    @pl.when(jnp.logical_and(i == 0, n_used > 0))
    def _():
        start_all(tok_ref, 0)

    @pl.when(i + 1 < n_used)
    def _():
        start_all(tok_next_ref, 1 - slot)

    @pl.when(i < n_used)
    def _():
        wait_all()
        o_ref[...] = buf_ref[slot].reshape(o_ref.shape).astype(o_ref.dtype)

    @pl.when(i >= n_used)
    def _():
        o_ref[...] = jnp.zeros(o_ref.shape, o_ref.dtype)


def _gather_rows(h, tok_slot, n_used, rows):
    p = tok_slot.shape[0]
    slab = h.shape[1:]
    d = slab[0] * slab[1]
    steps = p // rows
    tok3 = tok_slot.reshape(steps, 1, rows)
    grid_spec = pltpu.PrefetchScalarGridSpec(
        num_scalar_prefetch=1,
        grid=(steps,),
        in_specs=[pl.BlockSpec((1, 1, rows), lambda i, nu: (i, 0, 0), memory_space=pltpu.SMEM),
                  pl.BlockSpec((1, 1, rows), lambda i, nu: (jnp.minimum(i + 1, steps - 1), 0, 0),
                               memory_space=pltpu.SMEM),
                  pl.BlockSpec(memory_space=pl.ANY)],
        out_specs=pl.BlockSpec((rows, d), lambda i, nu: (i, 0)),
        scratch_shapes=[pltpu.VMEM((2, rows) + slab, F32), pltpu.SemaphoreType.DMA((2,))],
    )
    return pl.pallas_call(
        _gather_body,
        grid_spec=grid_spec,
        out_shape=jax.ShapeDtypeStruct((p, d), BF16),
        compiler_params=_params(("arbitrary",), 32),
        name="moe_gather",
    )(n_used, tok3, tok3, h)


def _segment_io(x_hbm, o_hbm, xbuf, obuf, zbuf, sem_x, sem_o, sem_z, pend, st0_ref, snt_ref, nut_ref,
                tm, tn, n_tiles, n_col_tiles):
    s, j = pl.program_id(0), pl.program_id(1)
    nt, t0 = snt_ref[s], st0_ref[s]
    n_slots = obuf.shape[0]

    def x_copy(r):
        return pltpu.make_async_copy(x_hbm.at[pl.ds((t0 + r) * tm, tm)], xbuf.at[pl.ds(r * tm, tm)], sem_x.at[r])

    def o_copy(slot, t, jj):
        return pltpu.make_async_copy(obuf.at[slot], o_hbm.at[pl.ds(t * tm, tm), pl.ds(jj * tn, tn)], sem_o.at[slot])

    def z_copy(t, jj):
        return pltpu.make_async_copy(zbuf, o_hbm.at[pl.ds(t * tm, tm), pl.ds(jj * tn, tn)], sem_z)

    @pl.when(jnp.logical_and(s == 0, j == 0))
    def _():
        for slot in range(n_slots):
            pend[slot] = 0

    @pl.when(j == 0)
    def _():
        def start(r, carry):
            x_copy(r).start()
            return carry

        lax.fori_loop(0, nt, start, 0)

    def rows(r, n):
        @pl.when(j == 0)
        def _():
            for t in range(n):
                x_copy(r + t).wait()

        return xbuf[pl.ds(pl.multiple_of(r * tm, tm), n * tm), :]

    def emit(r, value):
        slot = r % n_slots

        @pl.when(pend[slot] == 1)
        def _():
            o_copy(slot, 0, 0).wait()

        obuf[slot] = value.astype(obuf.dtype)
        o_copy(slot, t0 + r, j).start()
        pend[slot] = 1

    def finish():
        @pl.when(jnp.logical_and(s == pl.num_programs(0) - 1, j == pl.num_programs(1) - 1))
        def _():
            for slot in range(n_slots):
                @pl.when(pend[slot] == 1)
                def _(slot=slot):
                    o_copy(slot, 0, 0).wait()
                    pend[slot] = 0

            zbuf[...] = jnp.zeros(zbuf.shape, zbuf.dtype)

            def start(t, carry):
                for jj in range(n_col_tiles):
                    z_copy(t, jj).start()
                return carry

            def wait(t, carry):
                for jj in range(n_col_tiles):
                    z_copy(t, jj).wait()
                return carry

            lax.fori_loop(nut_ref[0], n_tiles, start, 0)
            lax.fori_loop(nut_ref[0], n_tiles, wait, 0)

    return nt, rows, emit, finish


def _for_row_tiles(nt, step):
    n_full = nt // ITER_TILES

    def full(i, carry):
        step(i * ITER_TILES, ITER_TILES)
        return carry

    lax.fori_loop(0, n_full, full, 0)
    done = n_full * ITER_TILES
    rem = nt - done
    n = ITER_TILES // 2
    while n >= 1:
        @pl.when((rem // n) % 2 == 1)
        def _(start=done + (rem // (2 * n)) * (2 * n), n=n):
            step(start, n)

        n //= 2


def _expert_up_body(se_ref, st0_ref, snt_ref, nut_ref, x_hbm, wg_ref, wl_ref, bg_ref, bl_ref, o_hbm,
                    xbuf, obuf, zbuf, sem_x, sem_o, sem_z, pend, *, tm, tn, n_tiles, n_col_tiles):
    nt, rows, emit, finish = _segment_io(x_hbm, o_hbm, xbuf, obuf, zbuf, sem_x, sem_o, sem_z, pend,
                                         st0_ref, snt_ref, nut_ref, tm, tn, n_tiles, n_col_tiles)

    @pl.when(nt > 0)
    def _():
        def step(r, n):
            x = rows(r, n)
            glu = jnp.minimum(_wdot(x, wg_ref[0]) + bg_ref[0], SWIGLU_LIMIT)
            lin = jnp.clip(_wdot(x, wl_ref[0]) + bl_ref[0], -SWIGLU_LIMIT, SWIGLU_LIMIT)
            act = glu * jax.nn.sigmoid(SWIGLU_ALPHA * glu) * (lin + 1.0)
            for t in range(n):
                emit(r + t, act[t * tm:(t + 1) * tm])

        _for_row_tiles(nt, step)

    finish()


def _expert_down_body(se_ref, st0_ref, snt_ref, nut_ref, a_hbm, w_ref, b_ref, o_hbm,
                      abuf, obuf, zbuf, sem_x, sem_o, sem_z, pend, *, tm, tn, n_tiles, n_col_tiles):
    nt, rows, emit, finish = _segment_io(a_hbm, o_hbm, abuf, obuf, zbuf, sem_x, sem_o, sem_z, pend,
                                         st0_ref, snt_ref, nut_ref, tm, tn, n_tiles, n_col_tiles)

    @pl.when(nt > 0)
    def _():
        def step(r, n):
            y = _wdot(rows(r, n), w_ref[0]) + b_ref[0]
            for t in range(n):
                emit(r + t, y[t * tm:(t + 1) * tm])

        _for_row_tiles(nt, step)

    finish()


def _expert_matmul(body, rows, weights, biases, col_offsets, n_cols, tn, out_dtype, route, tm, name):
    seg_e, seg_t0, seg_nt, n_used = route
    p, k = rows.shape
    n_seg = seg_e.shape[0]
    nj = n_cols // tn
    n_tiles = p // tm

    def w_index(off):
        return lambda s, j, se, st0, snt, nu: (se[s], 0, off + jnp.where(snt[s] > 0, j, nj - 1))

    in_specs = [pl.BlockSpec(memory_space=pl.ANY)]
    in_specs += [pl.BlockSpec((1, k, tn), w_index(off)) for off in col_offsets]
    in_specs += [pl.BlockSpec((1, 1, tn), w_index(off)) for off in col_offsets]
    scratch = [pltpu.VMEM((SEGMENT_TILES * tm, k), BF16),
               pltpu.VMEM((OUT_SLOTS, tm, tn), out_dtype),
               pltpu.VMEM((tm, tn), out_dtype),
               pltpu.SemaphoreType.DMA((SEGMENT_TILES,)), pltpu.SemaphoreType.DMA((OUT_SLOTS,)),
               pltpu.SemaphoreType.DMA, pltpu.SMEM((OUT_SLOTS,), jnp.int32)]
    grid_spec = pltpu.PrefetchScalarGridSpec(
        num_scalar_prefetch=4,
        grid=(n_seg, nj),
        in_specs=in_specs,
        out_specs=pl.BlockSpec(memory_space=pl.ANY),
        scratch_shapes=scratch,
    )
    return pl.pallas_call(
        functools.partial(body, tm=tm, tn=tn, n_tiles=n_tiles, n_col_tiles=nj),
        grid_spec=grid_spec,
        out_shape=jax.ShapeDtypeStruct((p, n_cols), out_dtype),
        compiler_params=_params(("arbitrary", "arbitrary"), 48),
        name=name,
    )(seg_e, seg_t0, seg_nt, n_used, rows, *weights, *biases)


def _expert_up(xs, w_up, b_up, route, tm):
    ne, _, two_de = w_up.shape
    de = two_de // 2
    tn = min(EXPERT_COL_TILE, de)
    b3 = b_up.reshape(ne, 1, two_de)
    return _expert_matmul(_expert_up_body, xs, (w_up, w_up), (b3, b3), (0, de // tn), de, tn, BF16, route, tm,
                          "expert_up")


def _expert_down(act, w_down, b_down, route, tm):
    ne, _, d = w_down.shape
    tn = min(COL_TILE, d)
    return _expert_matmul(_expert_down_body, act, (w_down,), (b_down.reshape(ne, 1, d),), (0,), d, tn, F32,
                          route, tm, "expert_down")


def _combine_body(pos_ref, pos_next_ref, y_hbm, x_ref, gate_ref, gp_ref, gs_ref, gf_ref, op_ref, os_ref,
                  buf_ref, sem, *, nbp):
    i = pl.program_id(0)
    tmc = x_ref.shape[0]
    slot = i % 2
    start_all, wait_all = _row_gather(pos_ref, y_hbm, buf_ref, sem, slot)

    @pl.when(i == 0)
    def _():
        start_all(pos_ref, 0)

    @pl.when(i + 1 < pl.num_programs(0))
    def _():
        start_all(pos_next_ref, 1 - slot)

    wait_all()
    gates = gate_ref[...]
    rows = buf_ref.at[slot]
    y = gates[:, 0:1] * rows[0:tmc, :]
    for k in range(1, TOP_K):
        y = y + gates[:, k:k + 1] * rows[k * tmc:(k + 1) * tmc, :]
    gt = jnp.where(i < nbp, gp_ref[0], gs_ref[...])
    x = x_ref[...] + gt * y
    out = x * lax.rsqrt(jnp.mean(x * x, axis=-1, keepdims=True) + EPS) * gf_ref[...]

    @pl.when(i < nbp)
    def _():
        op_ref[...] = out

    @pl.when(i >= nbp)
    def _():
        os_ref[...] = out


def _combine_final(y_sorted, pos, x, gates, gt_p, gt_s, g_final, tmc, nbp, t_prompt):
    m, d = x.shape
    steps = m // tmc
    tpb = t_prompt // tmc
    pos_blk = pos.reshape(steps, tmc, TOP_K).transpose(0, 2, 1).reshape(steps, 1, TOP_K * tmc)
    idx = pl.BlockSpec((1, 1, TOP_K * tmc), lambda i: (i, 0, 0), memory_space=pltpu.SMEM)
    idx_next = pl.BlockSpec((1, 1, TOP_K * tmc), lambda i: (jnp.minimum(i + 1, steps - 1), 0, 0),
                            memory_space=pltpu.SMEM)
    return pl.pallas_call(
        functools.partial(_combine_body, nbp=nbp),
        grid=(steps,),
        in_specs=[idx, idx_next,
                  pl.BlockSpec(memory_space=pl.ANY),
                  pl.BlockSpec((tmc, d), lambda i: (i, 0)),
                  pl.BlockSpec((tmc, LANES), lambda i: (i, 0)),
                  pl.BlockSpec((1, 1, d), lambda i: (jnp.minimum(i, nbp - 1) // tpb, 0, 0)),
                  pl.BlockSpec((tmc, d), lambda i: (jnp.maximum(i - nbp, 0), 0)),
                  pl.BlockSpec((1, d), lambda i: (0, 0))],
        out_specs=[pl.BlockSpec((tmc, d), lambda i: (jnp.minimum(i, nbp - 1), 0)),
                   pl.BlockSpec((tmc, d), lambda i: (jnp.maximum(i - nbp, 0), 0))],
        out_shape=[jax.ShapeDtypeStruct((nbp * tmc, d), F32), jax.ShapeDtypeStruct((m - nbp * tmc, d), F32)],
        scratch_shapes=[pltpu.VMEM((2, TOP_K * tmc, d), F32), pltpu.SemaphoreType.DMA((2,))],
        compiler_params=_params(("arbitrary",), 32),
        name="moe_combine_norm",
    )(pos_blk, pos_blk, y_sorted, x, gates, gt_p, gt_s, g_final.reshape(1, d))


def _route(top_idx, n_experts, tm):
    t = top_idx.shape[0]
    a = t * TOP_K
    experts = jnp.arange(n_experts, dtype=jnp.int32)
    flat_e = top_idx.reshape(a)
    order = jnp.argsort(flat_e, stable=True).astype(jnp.int32)
    rank = jnp.argsort(order).astype(jnp.int32)
    counts = jnp.sum((flat_e[:, None] == experts[None, :]).astype(jnp.int32), axis=0)
    tiles = (counts + tm - 1) // tm
    tile_end = jnp.cumsum(tiles)
    tile_start = tile_end - tiles
    start = jnp.cumsum(counts) - counts
    pos = tile_start[flat_e] * tm + rank - start[flat_e]
    n_tiles = -(-(-(-a // tm) + n_experts) // GATHER_TILES) * GATHER_TILES
    slot = jnp.arange(n_tiles * tm, dtype=jnp.int32)
    slot_e = jnp.minimum(jnp.sum((tile_end[None, :] * tm <= slot[:, None]).astype(jnp.int32), axis=1), n_experts - 1)
    offset = slot - tile_start[slot_e] * tm
    owned = jnp.logical_and(offset < counts[slot_e], slot < tile_end[-1] * tm)
    tok_slot = jnp.where(owned, order[jnp.clip(start[slot_e] + offset, 0, a - 1)] // TOP_K, 0)
    segs = (tiles + SEGMENT_TILES - 1) // SEGMENT_TILES
    seg_end = jnp.cumsum(segs)
    seg_start = seg_end - segs
    n_seg = n_experts + -(-n_tiles // SEGMENT_TILES)
    s = jnp.arange(n_seg, dtype=jnp.int32)
    s_live = jnp.minimum(s, seg_end[-1] - 1)
    seg_e = jnp.minimum(jnp.sum((seg_end[None, :] <= s_live[:, None]).astype(jnp.int32), axis=1), n_experts - 1)
    k = s_live - seg_start[seg_e]
    seg_nt = jnp.where(s < seg_end[-1], jnp.clip(tiles[seg_e] - k * SEGMENT_TILES, 0, SEGMENT_TILES), 0)
    seg_t0 = tile_start[seg_e] + k * SEGMENT_TILES

    def i32(v):
        return v.astype(jnp.int32)

    return i32(tok_slot), i32(pos), (i32(seg_e), i32(seg_t0), i32(seg_nt), i32(tile_end[-1]).reshape(1))


def _tile_rows(*lengths):
    tm = ROW_TILE
    while any(n % tm for n in lengths):
        tm //= 2
    return tm


def kernel(x_prompt, x_sample, state_pool, state_C, state_n, state_m, c_prompt, c_sample, w_ada, b_ada, g_norm1,
           w_in, b_in, w_pool_grp, pool_scale, w_proj_a, g_hnorm, w_proj_b, w_out, g_norm2, w_router, b_router,
           w_up, b_up, w_down, b_down, g_final):
    bp, tp, d = x_prompt.shape
    bs, ts, _ = x_sample.shape
    assert w_ada.shape[0] == 1, "one decoder layer is supported"
    sdt, pdt = state_C.dtype, state_pool.dtype
    (state_pool, state_C, state_n, state_m, w_ada, b_ada, g_norm1, w_in, b_in, w_pool_grp, pool_scale, w_proj_a,
     g_hnorm, w_proj_b, w_out, g_norm2, w_router, b_router, w_up, b_up, w_down, b_down) = [
        p.reshape(p.shape[1:]) for p in
        (state_pool, state_C, state_n, state_m, w_ada, b_ada, g_norm1, w_in, b_in, w_pool_grp, pool_scale,
         w_proj_a, g_hnorm, w_proj_b, w_out, g_norm2, w_router, b_router, w_up, b_up, w_down, b_down)]
    mp, ms = bp * tp, bs * ts
    tm = _tile_rows(tp, ms)
    tmc = min(ELEMENTWISE_ROWS, tm)
    nbp = mp // tm
    pool_w = w_pool_grp.shape[0] * w_pool_grp.shape[1]
    n_prev = state_pool.shape[1]
    dv = d // M_HEADS
    qk_w = w_in.shape[1] - pool_w - 2 * M_HEADS * dv - 2 * M_HEADS - 2 * d
    qk_w //= 2
    offs = {"u": 0, "q": pool_w, "k": pool_w + qk_w, "v": pool_w + 2 * qk_w}
    offs["o"] = offs["v"] + M_HEADS * dv
    offs["if"] = offs["o"] + M_HEADS * dv
    offs["gab"] = offs["if"] + 2 * M_HEADS

    c_all = jnp.concatenate([c_prompt, c_sample], axis=0)
    n_c = c_all.shape[0]
    c_all = jnp.pad(c_all, ((0, (-n_c) % 8), (0, 0)))
    mod = _ada(c_all, w_ada, b_ada)

    def mod_part(kind):
        part = mod[:, kind * d:(kind + 1) * d]
        return part[:bp].reshape(bp, 1, d), jnp.repeat(part[bp:n_c], ts, axis=0)

    (sh1_p, sh1_s), (sc1_p, sc1_s), (gt1_p, gt1_s) = mod_part(0), mod_part(1), mod_part(2)
    (sh2_p, sh2_s), (sc2_p, sc2_s), (gt2_p, gt2_s) = mod_part(3), mod_part(4), mod_part(5)

    h1, x_all = _norm1(x_prompt.reshape(mp, d), x_sample.reshape(ms, d), sc1_p, sh1_p, sc1_s, sh1_s,
                       g_norm1, tmc, tp)

    w_in_t = w_in.T

    def in_proj(name, n, dtype):
        return _linear_t(h1, w_in_t, b_in[offs[name]:offs[name] + n], offs[name], n, dtype, tm)

    u = in_proj("u", pool_w, F32)
    q = in_proj("q", qk_w, BF16)
    k = in_proj("k", qk_w, F32)
    v = in_proj("v", M_HEADS * dv, BF16)
    o = in_proj("o", M_HEADS * dv, F32)
    pre_if = in_proj("if", LANES, F32)
    zg = in_proj("gab", 2 * d, F32)

    a_p = _pool_prompt(u, w_pool_grp, pool_scale, bp, tp, tm)
    u_s = u[mp:].reshape(bs, ts, pool_w)
    padded_s = jnp.concatenate([state_pool, u_s], axis=1)
    a_s = _pool_sample(padded_s.transpose(1, 0, 2), w_pool_grp, pool_scale, ts)
    a_s = a_s.transpose(1, 0, 2).reshape(ms, pool_w)
    pool_p = jnp.stack([u[(b + 1) * tp - n_prev:(b + 1) * tp] for b in range(bp)])
    pool_s = padded_s[:, ts:, :]

    bm_p, c_p, n_p, m_p = _mlstm(q, k, v, o, pre_if, g_hnorm, bp, tp)

    def sample_rows(arr):
        return arr[mp:].reshape(bs, ts, arr.shape[1])

    bm_s, c_s, n_s, m_s = _mlstm(sample_rows(q), sample_rows(k), sample_rows(v), sample_rows(o),
                                 sample_rows(pre_if), g_hnorm, bs, ts,
                                 state=(state_C, state_n, state_m))

    a_all = jnp.concatenate([a_p, a_s], axis=0)
    bm_all = jnp.concatenate([bm_p, bm_s.reshape(ms, M_HEADS * dv)], axis=0)
    merged = _merge(a_all, bm_all, w_proj_a, w_proj_b, zg, tm)
    x1 = _outproj(merged, w_out, x_all, gt1_p, gt1_s, tm, nbp, tp)

    h2, top_idx, gates = _norm2_router(x1, sc2_p, sh2_p, sc2_s, sh2_s, g_norm2, w_router, b_router,
                                       tmc, mp // tmc, tp)
    n_experts = w_router.shape[1]
    te = EXPERT_ROW_TILE
    tok_slot, pos, route = _route(top_idx[:, :TOP_K], n_experts, te)
    gather_steps = (route[3] + GATHER_TILES - 1) // GATHER_TILES
    xs = _gather_rows(h2, tok_slot, gather_steps, GATHER_TILES * te)
    act = _expert_up(xs, w_up, b_up, route, te)
    y_sorted = _expert_down(act, w_down, b_down, route, te)
    y_p, y_s = _combine_final(y_sorted, pos, x1, gates, gt2_p, gt2_s, g_final, tmc, mp // tmc, tp)

    return (y_p.reshape(bp, tp, d), y_s.reshape(bs, ts, d),
            pool_p.astype(pdt)[None], pool_s.astype(pdt)[None],
            c_p.astype(sdt)[None], c_s.astype(sdt)[None],
            n_p.astype(sdt)[None], n_s.astype(sdt)[None],
            m_p.astype(sdt)[None], m_s.astype(sdt)[None])
```
